```python
import math
import jax, jax.numpy as jnp
from jax import lax
import numpy as np

D_MODEL = 2048
BATCH = 2
SEQ = 16384
DEPTH = 4
DEC_BATCH = 32
DEC_SEQ = 32
PAST_LEN = 2048

CHUNK = 64
N_MIXERS = 3
N_A = (DEPTH + 2) // 3
N_B = (DEPTH + 1) // 3
N_C = DEPTH // 3

SB_HEADS = 8
SB_HEAD_DIM = 128
SB_WIDTH = SB_HEADS * SB_HEAD_DIM
SB_QBLOCK = 128
SB_KBLOCK = 128
SWA_HEADS = 32
SWA_KV_HEADS = 4
SWA_HEAD_DIM = D_MODEL // SWA_HEADS
SWA_REP = SWA_HEADS // SWA_KV_HEADS
SWA_WIDTH = SWA_HEADS * SWA_HEAD_DIM
SWA_KV_WIDTH = SWA_KV_HEADS * SWA_HEAD_DIM
WINDOW = 128
WIN_CHUNKS = WINDOW // CHUNK
SSM_WIDTH = D_MODEL // 2
SSM_GROUP = 16
SSM_GROUPS = SSM_WIDTH // SSM_GROUP
SSM_STATE = 64
DT_MIN = 1e-3
DT_MAX = 1e-1

ALPHA = (2 * DEPTH) ** 0.25
BETA = (8 * DEPTH) ** -0.25
LN_EPS = 1e-5

kernel_name = "hybrid_stream_encoder_step"


def layer_norm(x, g, b):
    xf = x.astype(jnp.float32)
    mu = jnp.mean(xf, axis=-1, keepdims=True)
    xc = xf - mu
    var = jnp.mean(xc * xc, axis=-1, keepdims=True)
    return (xc * lax.rsqrt(var + LN_EPS) * g.astype(jnp.float32) + b.astype(jnp.float32)).astype(x.dtype)


def stick_breaking(q, k, v, q_pos, k_pos):
    z = jnp.einsum('bqhd,bnkhd->bhqnk', q, k).astype(jnp.float32) * (SB_HEAD_DIM ** -0.5)
    mask = k_pos[None, :, :] < q_pos[:, None, None]
    log_1mb = jnp.where(mask, jax.nn.log_sigmoid(-z), 0.0)
    tri = jnp.tril(jnp.ones((SB_KBLOCK, SB_KBLOCK), jnp.float32))
    r_in = jnp.einsum('bhqnk,kj->bhqnj', log_1mb, tri, precision=lax.Precision.HIGHEST)
    tot = r_in[..., 0]
    later = lax.cumsum(tot, axis=3, reverse=True) - tot
    w = jnp.where(mask, jnp.exp(z + r_in + later[..., None]), 0.0)
    return jnp.einsum('bhqnk,bnkhd->bqhd', w.astype(v.dtype), v)


def sb_project(x, w_in):
    q, k, v, g = jnp.split(x @ w_in, 4, axis=-1)
    hs = x.shape[:-1] + (SB_HEADS, SB_HEAD_DIM)
    return q.reshape(hs), k.reshape(hs), v.reshape(hs), g


def sb_prompt(x, w_in, w_out):
    b, s, _ = x.shape
    q, k, v, g = sb_project(x, w_in)
    outs = []
    for i in range(s // SB_QBLOCK):
        lo, hi = i * SB_QBLOCK, (i + 1) * SB_QBLOCK
        nk = hi // SB_KBLOCK
        kb = k[:, :hi].reshape(b, nk, SB_KBLOCK, SB_HEADS, SB_HEAD_DIM)
        vb = v[:, :hi].reshape(b, nk, SB_KBLOCK, SB_HEADS, SB_HEAD_DIM)
        q_pos = jnp.arange(lo, hi, dtype=jnp.int32)
        k_pos = jnp.arange(hi, dtype=jnp.int32).reshape(nk, SB_KBLOCK)
        outs.append(stick_breaking(q[:, lo:hi], kb, vb, q_pos, k_pos))
    o = jnp.concatenate(outs, axis=1).reshape(b, s, SB_WIDTH)
    return (o * jax.nn.silu(g)) @ w_out, k, v


def sb_sample(x, cache_k, cache_v, w_in, w_out):
    b, t, _ = x.shape
    past = cache_k.shape[1]
    q, k, v, g = sb_project(x, w_in)
    n = past + t
    n_pad = -(-n // SB_KBLOCK) * SB_KBLOCK
    nk = n_pad // SB_KBLOCK
    pad = ((0, 0), (0, n_pad - n), (0, 0), (0, 0))
    keys = jnp.pad(jnp.concatenate([cache_k.astype(k.dtype), k], axis=1), pad)
    vals = jnp.pad(jnp.concatenate([cache_v.astype(v.dtype), v], axis=1), pad)
    keys = keys.reshape(b, nk, SB_KBLOCK, SB_HEADS, SB_HEAD_DIM)
    vals = vals.reshape(b, nk, SB_KBLOCK, SB_HEADS, SB_HEAD_DIM)
    q_pos = past + jnp.arange(t, dtype=jnp.int32)
    k_pos = jnp.arange(n_pad, dtype=jnp.int32).reshape(nk, SB_KBLOCK)
    o = stick_breaking(q, keys, vals, q_pos, k_pos).reshape(b, t, SB_WIDTH)
    return (o * jax.nn.silu(g)) @ w_out, k, v


def alibi_slopes(n):
    return 2.0 ** (-8.0 * jnp.arange(1, n + 1, dtype=jnp.float32) / n)


def sink_attention(q, k, v, bias, valid, sinks):
    logits = jnp.einsum('bcqkrd,bcskd->bckrqs', q, k).astype(jnp.float32) * (SWA_HEAD_DIM ** -0.5) + bias
    logits = jnp.where(valid, logits, -jnp.inf)
    sink = jnp.broadcast_to(sinks.astype(jnp.float32)[:, :, None, None], logits.shape[:-1] + (1,))
    p = jax.nn.softmax(jnp.concatenate([logits, sink], axis=-1), axis=-1)[..., :-1]
    return jnp.einsum('bckrqs,bcskd->bcqkrd', p.astype(v.dtype), v)


def swa_project(x, w_in):
    q, k, v, g = jnp.split(x @ w_in, [SWA_WIDTH, SWA_WIDTH + SWA_KV_WIDTH, SWA_WIDTH + 2 * SWA_KV_WIDTH], axis=-1)
    lead = x.shape[:-1]
    q = q.reshape(lead + (SWA_KV_HEADS, SWA_REP, SWA_HEAD_DIM))
    k = k.reshape(lead + (SWA_KV_HEADS, SWA_HEAD_DIM))
    v = v.reshape(lead + (SWA_KV_HEADS, SWA_HEAD_DIM))
    return q, k, v, g


def swa_prompt(x, w_in, sinks, w_out):
    b, s, _ = x.shape
    nc = s // CHUNK
    span = (WIN_CHUNKS + 1) * CHUNK
    q, k, v, g = swa_project(x, w_in)

    def band(t):
        tp = jnp.pad(t, ((0, 0), (WIN_CHUNKS * CHUNK, 0), (0, 0), (0, 0)))
        tp = tp.reshape(b, nc + WIN_CHUNKS, CHUNK, SWA_KV_HEADS, SWA_HEAD_DIM)
        return jnp.concatenate([tp[:, i:i + nc] for i in range(WIN_CHUNKS + 1)], axis=2)

    qc = q.reshape(b, nc, CHUNK, SWA_KV_HEADS, SWA_REP, SWA_HEAD_DIM)
    qi = jnp.arange(CHUNK, dtype=jnp.int32)
    kj = jnp.arange(span, dtype=jnp.int32)
    dist = jnp.abs(qi[:, None] + WIN_CHUNKS * CHUNK - kj[None, :]).astype(jnp.float32)
    bias = -alibi_slopes(SWA_HEADS).reshape(SWA_KV_HEADS, SWA_REP)[:, :, None, None] * dist
    k_pos = jnp.arange(nc, dtype=jnp.int32)[:, None] * CHUNK - WIN_CHUNKS * CHUNK + kj[None, :]
    valid = (k_pos >= 0)[None, :, None, None, None, :]
    o = sink_attention(qc, band(k), band(v), bias, valid, sinks).reshape(b, s, SWA_WIDTH)
    win = min(WINDOW, s)
    return (o * jax.nn.silu(g)) @ w_out, k[:, s - win:], v[:, s - win:]


def swa_sample(x, cache_k, cache_v, w_in, sinks, w_out):
    b, t, _ = x.shape
    w = cache_k.shape[1]
    q, k, v, g = swa_project(x, w_in)
    keys = jnp.concatenate([cache_k.astype(k.dtype), k], axis=1)[:, None]
    vals = jnp.concatenate([cache_v.astype(v.dtype), v], axis=1)[:, None]
    k_rel = jnp.concatenate([jnp.arange(w, dtype=jnp.int32) - w, jnp.arange(t, dtype=jnp.int32)])
    dist = jnp.abs(jnp.arange(t, dtype=jnp.int32)[:, None] - k_rel[None, :]).astype(jnp.float32)
    bias = -alibi_slopes(SWA_HEADS).reshape(SWA_KV_HEADS, SWA_REP)[:, :, None, None] * dist
    o = sink_attention(q[:, None], keys, vals, bias, True, sinks).reshape(b, t, SWA_WIDTH)
    return (o * jax.nn.silu(g)) @ w_out, k, v


def ssm_discretize(a_re, a_im, log_dt, b_re, b_im):
    a_re = jnp.minimum(a_re.astype(jnp.float32), -1e-4)
    a_im = a_im.astype(jnp.float32)
    dt = jnp.exp(log_dt.astype(jnp.float32))[:, None]
    mag = jnp.exp(dt * a_re)
    lb_re = mag * jnp.cos(dt * a_im)
    lb_im = mag * jnp.sin(dt * a_im)
    n_re = lb_re - 1.0
    den = a_re * a_re + a_im * a_im
    f_re = ((n_re * a_re + lb_im * a_im) / den)[..., None]
    f_im = ((lb_im * a_re - n_re * a_im) / den)[..., None]
    br = b_re.astype(jnp.float32)
    bi = b_im.astype(jnp.float32)
    return lb_re, lb_im, f_re * br - f_im * bi, f_re * bi + f_im * br


def complex_linear_combine(e1, e2):
    a1r, a1i, b1r, b1i = e1
    a2r, a2i, b2r, b2i = e2
    return (a2r * a1r - a2i * a1i, a2r * a1i + a2i * a1r,
            a2r * b1r - a2i * b1i + b2r, a2r * b1i + a2i * b1r + b2i)


def s5_block(uc, h_re, h_im, lb_re, lb_im, bb_re, bb_im, c_re, c_im):
    bu_re = jnp.einsum('btgc,gpc->btgp', uc, bb_re)
    bu_im = jnp.einsum('btgc,gpc->btgp', uc, bb_im)
    lam_re = jnp.broadcast_to(lb_re, bu_re.shape)
    lam_im = jnp.broadcast_to(lb_im, bu_im.shape)
    pw_re, pw_im, s_re, s_im = lax.associative_scan(complex_linear_combine, (lam_re, lam_im, bu_re, bu_im), axis=1)
    hr = s_re + pw_re * h_re[:, None] - pw_im * h_im[:, None]
    hi = s_im + pw_re * h_im[:, None] + pw_im * h_re[:, None]
    y = jnp.einsum('btgp,gcp->btgc', hr, c_re) - jnp.einsum('btgp,gcp->btgc', hi, c_im)
    return y, hr[:, -1], hi[:, -1]


def s5_mixer(x, h_re, h_im, w_in, a_re, a_im, log_dt, b_re, b_im, c_re, c_im, d, w_glu, w_out):
    b, t, _ = x.shape
    u, g = jnp.split(x @ w_in, 2, axis=-1)
    lb_re, lb_im, bb_re, bb_im = ssm_discretize(a_re, a_im, log_dt, b_re, b_im)
    cr = c_re.astype(jnp.float32)
    ci = c_im.astype(jnp.float32)
    uf = u.astype(jnp.float32)
    blk = min(t, CHUNK)
    n_blk = t // blk
    ub = jnp.moveaxis(uf.reshape(b, n_blk, blk, SSM_GROUPS, SSM_GROUP), 1, 0)

    def step(h, uc):
        y, hr, hi = s5_block(uc, h[0], h[1], lb_re, lb_im, bb_re, bb_im, cr, ci)
        return (hr, hi), y

    (h_re, h_im), yb = lax.scan(step, (h_re.astype(jnp.float32), h_im.astype(jnp.float32)), ub)
    y = jnp.moveaxis(yb, 0, 1).reshape(b, t, SSM_WIDTH) + d.astype(jnp.float32) * uf
    y = jax.nn.gelu(y).astype(x.dtype)
    y = y * jax.nn.sigmoid(y @ w_glu)
    return (y * jax.nn.silu(g)) @ w_out, h_re, h_im


def setup_inputs(seed: int = 0) -> dict:
    key = jax.random.key(seed)
    ks = jax.random.split(key, 26)
    f32 = jnp.float32

    def nrm(k, shape, scale):
        return jax.random.normal(k, shape, f32) * scale

    win = min(WINDOW, PAST_LEN)
    n_in_a = 4 * SB_WIDTH
    n_in_b = SWA_WIDTH + 2 * SWA_KV_WIDTH + SWA_WIDTH
    n_in_c = 2 * SSM_WIDTH
    a_im = jnp.pi * jnp.arange(SSM_STATE, dtype=f32) + nrm(ks[17], (N_C, SSM_GROUPS, SSM_STATE), 0.01)
    return {
        "x_prompt": nrm(ks[0], (BATCH, SEQ, D_MODEL), 1.0),
        "x_sample": nrm(ks[1], (DEC_BATCH, DEC_SEQ, D_MODEL), 1.0),
        "cache_k_a": nrm(ks[2], (N_A, DEC_BATCH, PAST_LEN, SB_HEADS, SB_HEAD_DIM), 1.0),
        "cache_v_a": nrm(ks[3], (N_A, DEC_BATCH, PAST_LEN, SB_HEADS, SB_HEAD_DIM), 1.0),
        "cache_k_b": nrm(ks[4], (N_B, DEC_BATCH, win, SWA_KV_HEADS, SWA_HEAD_DIM), 1.0),
        "cache_v_b": nrm(ks[5], (N_B, DEC_BATCH, win, SWA_KV_HEADS, SWA_HEAD_DIM), 1.0),
        "state_re_c": nrm(ks[6], (N_C, DEC_BATCH, SSM_GROUPS, SSM_STATE), 0.1),
        "state_im_c": nrm(ks[7], (N_C, DEC_BATCH, SSM_GROUPS, SSM_STATE), 0.1),
        "ln_g": 1.0 + nrm(ks[8], (DEPTH, D_MODEL), 0.02),
        "ln_b": nrm(ks[9], (DEPTH, D_MODEL), 0.02),
        "w_in_a": nrm(ks[10], (N_A, D_MODEL, n_in_a), D_MODEL ** -0.5),
        "w_out_a": nrm(ks[11], (N_A, SB_WIDTH, D_MODEL), BETA * SB_WIDTH ** -0.5),
        "w_in_b": nrm(ks[12], (N_B, D_MODEL, n_in_b), D_MODEL ** -0.5),
        "sinks_b": nrm(ks[13], (N_B, SWA_KV_HEADS, SWA_REP), 1.0),
        "w_out_b": nrm(ks[14], (N_B, SWA_WIDTH, D_MODEL), BETA * SWA_WIDTH ** -0.5),
        "w_in_c": nrm(ks[15], (N_C, D_MODEL, n_in_c), D_MODEL ** -0.5),
        "a_re_c": -0.5 + nrm(ks[16], (N_C, SSM_GROUPS, SSM_STATE), 0.01),
        "a_im_c": a_im,
        "log_dt_c": jax.random.uniform(ks[18], (N_C, SSM_GROUPS), f32, math.log(DT_MIN), math.log(DT_MAX)),
        "b_re_c": nrm(ks[19], (N_C, SSM_GROUPS, SSM_STATE, SSM_GROUP), (2 * SSM_GROUP) ** -0.5),
        "b_im_c": nrm(ks[20], (N_C, SSM_GROUPS, SSM_STATE, SSM_GROUP), (2 * SSM_GROUP) ** -0.5),
        "c_re_c": nrm(ks[21], (N_C, SSM_GROUPS, SSM_GROUP, SSM_STATE), (2 * SSM_STATE) ** -0.5),
        "c_im_c": nrm(ks[22], (N_C, SSM_GROUPS, SSM_GROUP, SSM_STATE), (2 * SSM_STATE) ** -0.5),
        "d_c": nrm(ks[23], (N_C, SSM_WIDTH), 1.0),
        "w_glu_c": nrm(ks[24], (N_C, SSM_WIDTH, SSM_WIDTH), SSM_WIDTH ** -0.5),
        "w_out_c": nrm(ks[25], (N_C, SSM_WIDTH, D_MODEL), BETA * SSM_WIDTH ** -0.5),
    }


def reference(x_prompt, x_sample, cache_k_a, cache_v_a, cache_k_b, cache_v_b, state_re_c, state_im_c,
              ln_g, ln_b, w_in_a, w_out_a, w_in_b, sinks_b, w_out_b,
              w_in_c, a_re_c, a_im_c, log_dt_c, b_re_c, b_im_c, c_re_c, c_im_c, d_c, w_glu_c, w_out_c):
    xp, xs = x_prompt, x_sample
    ka_p, va_p, ka_s, va_s = [], [], [], []
    kb_p, vb_p, kb_s, vb_s = [], [], [], []
    hr_p, hi_p, hr_s, hi_s = [], [], [], []
    for i in range(DEPTH):
        j = i // N_MIXERS
        kind = i % N_MIXERS
        if kind == 0:
            yp, kp, vp = sb_prompt(xp, w_in_a[j], w_out_a[j])
            ys, kn, vn = sb_sample(xs, cache_k_a[j], cache_v_a[j], w_in_a[j], w_out_a[j])
            ka_p.append(kp); va_p.append(vp); ka_s.append(kn); va_s.append(vn)
        elif kind == 1:
            yp, kp, vp = swa_prompt(xp, w_in_b[j], sinks_b[j], w_out_b[j])
            ys, kn, vn = swa_sample(xs, cache_k_b[j], cache_v_b[j], w_in_b[j], sinks_b[j], w_out_b[j])
            kb_p.append(kp); vb_p.append(vp); kb_s.append(kn); vb_s.append(vn)
        else:
            z0 = jnp.zeros((xp.shape[0], SSM_GROUPS, SSM_STATE), jnp.float32)
            yp, hrp, hip = s5_mixer(xp, z0, z0, w_in_c[j], a_re_c[j], a_im_c[j], log_dt_c[j], b_re_c[j], b_im_c[j],
                                    c_re_c[j], c_im_c[j], d_c[j], w_glu_c[j], w_out_c[j])
            ys, hrs, his = s5_mixer(xs, state_re_c[j], state_im_c[j], w_in_c[j], a_re_c[j], a_im_c[j], log_dt_c[j],
                                    b_re_c[j], b_im_c[j], c_re_c[j], c_im_c[j], d_c[j], w_glu_c[j], w_out_c[j])
            hr_p.append(hrp); hi_p.append(hip); hr_s.append(hrs); hi_s.append(his)
        xp = layer_norm(ALPHA * xp + yp, ln_g[i], ln_b[i])
        xs = layer_norm(ALPHA * xs + ys, ln_g[i], ln_b[i])
    return (xp, xs,
            jnp.stack(ka_p), jnp.stack(va_p), jnp.stack(ka_s), jnp.stack(va_s),
            jnp.stack(kb_p), jnp.stack(vb_p), jnp.stack(kb_s), jnp.stack(vb_s),
            jnp.stack(hr_p), jnp.stack(hi_p), jnp.stack(hr_s), jnp.stack(hi_s))
```

```python
import functools
import math

import jax
import jax.numpy as jnp
from jax import lax
from jax.experimental import pallas as pl
from jax.experimental.pallas import tpu as pltpu

F32 = jnp.float32
BF16 = jnp.bfloat16

CHUNK = 64
WINDOW = 128
N_MIXERS = 3
SSM_GROUP = 16
LN_EPS = 1e-5
SB_KBLOCK = 128

LANES = 128
VMEM_LIMIT = 56 * 1024 * 1024

SB_DEAD_LOG = -104.0
SSM_L = 8
SSM_SLAB_GROUPS = LANES // SSM_GROUP


def _silu(x):
    return x * (1.0 / (1.0 + jnp.exp(-x)))


def _cparams(sem, vmem=VMEM_LIMIT):
    return pltpu.CompilerParams(dimension_semantics=sem, vmem_limit_bytes=vmem)


def _proj_kernel(x_ref, w_ref, *out_refs, segs, step):
    xb = x_ref[...].astype(BF16)
    outs = iter(out_refs)
    for c0, width, scale, dtypes, slab in segs:
        o_refs = [next(outs) for _ in dtypes]
        for s0 in range(0, width, step):
            s1 = min(width, s0 + step)
            acc = jnp.dot(xb, w_ref[:, c0 + s0:c0 + s1], preferred_element_type=F32)
            if scale != 1.0:
                acc = acc * scale
            for o_ref in o_refs:
                if slab:
                    for l0 in range(s0, s1, LANES):
                        o_ref[l0 // LANES] = acc[:, l0 - s0:l0 - s0 + LANES].astype(o_ref.dtype)
                else:
                    o_ref[:, s0:s1] = acc.astype(o_ref.dtype)


def _proj(x2d, w_bf16, segs, tm, name):
    t, d = x2d.shape
    n = w_bf16.shape[1]
    tm = min(tm, t)
    assert t % tm == 0
    out_specs, out_shape = [], []
    for _, wd, _, dtypes, slab in segs:
        for dt in dtypes:
            if slab:
                out_specs.append(pl.BlockSpec((wd // LANES, tm, LANES), lambda i: (0, i, 0)))
                out_shape.append(jax.ShapeDtypeStruct((wd // LANES, t, LANES), dt))
            else:
                out_specs.append(pl.BlockSpec((tm, wd), lambda i: (i, 0)))
                out_shape.append(jax.ShapeDtypeStruct((t, wd), dt))
    kern = functools.partial(_proj_kernel, segs=segs, step=1024)
    return pl.pallas_call(
        kern,
        grid=(t // tm,),
        in_specs=[
            pl.BlockSpec((tm, d), lambda i: (i, 0)),
            pl.BlockSpec((d, n), lambda i: (0, 0), pipeline_mode=pl.Buffered(1)),
        ],
        out_specs=out_specs,
        out_shape=out_shape,
        compiler_params=_cparams(("parallel",)),
        name=name,
    )(x2d, w_bf16)


def _layer_norm_rows(r, g, b):
    mu = jnp.mean(r, axis=-1, keepdims=True)
    rc = r - mu
    var = jnp.mean(rc * rc, axis=-1, keepdims=True)
    return rc * lax.rsqrt(var + LN_EPS) * g + b


def _outproj_kernel(a_ref, w_ref, x_ref, g_ref, b_ref, o_ref, *, alpha):
    y = jnp.dot(a_ref[...], w_ref[...], preferred_element_type=F32)
    r = alpha * x_ref[...] + y
    o_ref[...] = _layer_norm_rows(r, g_ref[...], b_ref[...])


def _outproj_ln(a2d, w_bf16, x2d, ln_g, ln_b, alpha, tm, name):
    t, wdt = a2d.shape
    d = x2d.shape[1]
    tm = min(tm, t)
    assert t % tm == 0
    return pl.pallas_call(
        functools.partial(_outproj_kernel, alpha=alpha),
        grid=(t // tm,),
        in_specs=[
            pl.BlockSpec((tm, wdt), lambda i: (i, 0)),
            pl.BlockSpec((wdt, d), lambda i: (0, 0), pipeline_mode=pl.Buffered(1)),
            pl.BlockSpec((tm, d), lambda i: (i, 0)),
            pl.BlockSpec((1, d), lambda i: (0, 0)),
            pl.BlockSpec((1, d), lambda i: (0, 0)),
        ],
        out_specs=pl.BlockSpec((tm, d), lambda i: (i, 0)),
        out_shape=jax.ShapeDtypeStruct((t, d), F32),
        compiler_params=_cparams(("parallel",)),
        name=name,
    )(a2d, w_bf16, x2d, ln_g.reshape(1, d).astype(F32), ln_b.reshape(1, d).astype(F32))


def _sb_kernel(q_ref, k_ref, v_ref, g_ref, tri_ref, o_ref, *, qb, nchain, q_pos0):
    kb = SB_KBLOCK
    i = pl.program_id(2)
    tq = qb * nchain
    pos0 = [q_pos0 + i * tq + c * qb for c in range(nchain)]
    n0 = [p // kb for p in pos0]
    n0_max = n0[-1]

    col = lax.broadcasted_iota(jnp.int32, (qb, kb), 1)
    row = lax.broadcasted_iota(jnp.int32, (qb, kb), 0)

    def block(c, nb, later, acc, diag):
        off = pl.multiple_of(jnp.maximum(nb, 0) * kb, kb)
        kblk = k_ref[0, pl.ds(off, kb), :]
        vblk = v_ref[0, pl.ds(off, kb), :]
        q = q_ref[0, c * qb:(c + 1) * qb, :]
        z = lax.dot_general(q, kblk, (((1,), (1,)), ((), ())), preferred_element_type=F32)
        sp = jnp.maximum(z, 0.0) + jnp.log(1.0 + jnp.exp(-jnp.abs(z)))
        l1m = -sp
        lbeta = z - sp
        if diag:
            mask = (nb * kb + col) < (pos0[c] + row)
            l1m = jnp.where(mask, l1m, 0.0)
        hi = l1m.astype(BF16)
        r1 = l1m - hi.astype(F32)
        mid = r1.astype(BF16)
        lo = (r1 - mid.astype(F32)).astype(BF16)
        cs = jnp.dot(jnp.concatenate([hi, mid, lo], axis=1), tri_ref[...], preferred_element_type=F32)
        rex = cs[:, :kb]
        tot = cs[:, kb:]
        w = jnp.exp(lbeta + rex + later)
        if diag:
            w = jnp.where(mask, w, 0.0)
        acc = acc + jnp.dot(w.astype(BF16), vblk, preferred_element_type=F32)
        return later + tot, acc

    zero = jnp.zeros((qb, kb), F32)
    laters, accs = [], []
    for c in range(nchain):
        l, a = block(c, n0[c], zero, zero, True)
        laters.append(l)
        accs.append(a)

    def alive(ls):
        m = ls[0]
        for l in ls[1:]:
            m = jnp.maximum(m, l)
        return jnp.max(m) > SB_DEAD_LOG

    def cond(st):
        t, live = st[0], st[1]
        return jnp.logical_and(t <= n0_max, live)

    def body(st):
        t = st[0]
        ls, as_ = list(st[2]), list(st[3])
        for c in range(nchain):
            nb = n0[c] - t
            l_in = ls[c] + jnp.where(nb >= 0, 0.0, -1e30).astype(F32)
            ls[c], as_[c] = block(c, nb, l_in, as_[c], False)
        return (t + 1, alive(ls), tuple(ls), tuple(as_))

    st = lax.while_loop(cond, body, (jnp.int32(1), alive(laters), tuple(laters), tuple(accs)))
    accs = st[3]
    for c in range(nchain):
        g = g_ref[0, c * qb:(c + 1) * qb, :].astype(F32)
        o_ref[0, c * qb:(c + 1) * qb, :] = (accs[c] * _silu(g)).astype(o_ref.dtype)


def _sb_tri():
    j = jnp.arange(SB_KBLOCK)
    later = (j[:, None] > j[None, :]).astype(BF16)
    ones = jnp.ones((SB_KBLOCK, SB_KBLOCK), BF16)
    one = jnp.concatenate([later, ones], axis=1)
    return jnp.concatenate([one, one, one], axis=0)


def _sb_attention(q, k, v, g, heads, qb, nchain, q_pos0, name):
    b, sq, w = q.shape
    sk = k.shape[1]
    hd = w // heads
    assert hd == SB_KBLOCK and sk % SB_KBLOCK == 0
    assert SB_KBLOCK % qb == 0 and q_pos0 % SB_KBLOCK == 0
    tq = qb * nchain
    assert sq % tq == 0 and (tq % SB_KBLOCK == 0 or sq == tq)
    kern = functools.partial(_sb_kernel, qb=qb, nchain=nchain, q_pos0=q_pos0)
    return pl.pallas_call(
        kern,
        grid=(b, heads, sq // tq),
        in_specs=[
            pl.BlockSpec((1, tq, hd), lambda bi, h, i: (bi, i, h)),
            pl.BlockSpec((1, sk, hd), lambda bi, h, i: (bi, 0, h)),
            pl.BlockSpec((1, sk, hd), lambda bi, h, i: (bi, 0, h)),
            pl.BlockSpec((1, tq, hd), lambda bi, h, i: (bi, i, h)),
            pl.BlockSpec((3 * SB_KBLOCK, 2 * SB_KBLOCK), lambda bi, h, i: (0, 0)),
        ],
        out_specs=pl.BlockSpec((1, tq, hd), lambda bi, h, i: (bi, i, h)),
        out_shape=jax.ShapeDtypeStruct((b, sq, w), BF16),
        compiler_params=_cparams(("parallel", "parallel", "arbitrary")),
        name=name,
    )(q, k, v, g, _sb_tri())


def _swa_kernel(sink_ref, q_ref, kp_ref, kc_ref, vp_ref, vc_ref, g_ref, bias_ref, o_ref,
                *, cq, nch, span, kvh, rep, hd, mask_first):
    i = pl.program_id(1)
    kbuf = jnp.concatenate([kp_ref[0], kc_ref[0]], axis=0)
    vbuf = jnp.concatenate([vp_ref[0], vc_ref[0]], axis=0)
    nprev = kp_ref.shape[1]
    gw = rep * hd
    for kv in range(kvh):
        kk = kbuf[:, kv * hd:(kv + 1) * hd]
        vv = vbuf[:, kv * hd:(kv + 1) * hd]
        bias = bias_ref[kv]
        sink = jnp.concatenate(
            [jnp.full((cq, 1), sink_ref[kv, r], F32) for r in range(rep)], axis=0)
        for j in range(nch):
            ks = kk[j * cq:j * cq + span]
            vs = vv[j * cq:j * cq + span]
            qs = q_ref[0, j * cq:(j + 1) * cq, kv * gw:(kv + 1) * gw]
            lhs = jnp.concatenate([qs[:, r * hd:(r + 1) * hd] for r in range(rep)], axis=0)
            logits = lax.dot_general(lhs, ks, (((1,), (1,)), ((), ())), preferred_element_type=F32) + bias
            if mask_first and j * cq < nprev:
                buf_row = j * cq + lax.broadcasted_iota(jnp.int32, logits.shape, 1)
                valid = jnp.logical_or(i > 0, buf_row >= nprev)
                logits = jnp.where(valid, logits, -jnp.inf)
            m = jnp.maximum(jnp.max(logits, axis=-1, keepdims=True), sink)
            p = jnp.exp(logits - m)
            den = jnp.sum(p, axis=-1, keepdims=True) + jnp.exp(sink - m)
            o = jnp.dot(p.astype(BF16), vs, preferred_element_type=F32) * (1.0 / den)
            ot = jnp.concatenate([o[r * cq:(r + 1) * cq] for r in range(rep)], axis=1)
            g = g_ref[0, j * cq:(j + 1) * cq, kv * gw:(kv + 1) * gw].astype(F32)
            o_ref[0, j * cq:(j + 1) * cq, kv * gw:(kv + 1) * gw] = (ot * _silu(g)).astype(o_ref.dtype)


def _alibi_bias(kvh, rep, cq, span, nprev):
    n = kvh * rep
    slopes = 2.0 ** (-8.0 * jnp.arange(1, n + 1, dtype=F32) / n)
    qi = jnp.arange(cq, dtype=jnp.int32)
    kj = jnp.arange(span, dtype=jnp.int32)
    dist = jnp.abs(qi[:, None] + nprev - kj[None, :]).astype(F32)
    bias = -slopes.reshape(kvh, rep)[:, :, None, None] * dist
    return bias.reshape(kvh, rep * cq, span)


def _swa_attention(q, k_prev_src, k_cur, v_prev_src, v_cur, g, sinks, *, cq, nch, nprev, prev_from_cur, name):
    b, sq, qw = q.shape
    kvw = k_cur.shape[2]
    kvh, rep = sinks.shape
    hd = kvw // kvh
    rows = cq * nch
    span = nprev + cq
    assert sq % rows == 0 and qw == kvh * rep * hd
    if prev_from_cur:
        assert rows % nprev == 0
        ratio = rows // nprev
        prev_map = lambda bi, i: (bi, jnp.maximum(i * ratio - 1, 0), 0)
    else:
        assert sq == rows
        prev_map = lambda bi, i: (bi, 0, 0)
    cur_map = lambda bi, i: (bi, i, 0)
    kern = functools.partial(_swa_kernel, cq=cq, nch=nch, span=span, kvh=kvh, rep=rep, hd=hd,
                             mask_first=prev_from_cur)
    return pl.pallas_call(
        kern,
        grid=(b, sq // rows),
        in_specs=[
            pl.BlockSpec(memory_space=pltpu.SMEM),
            pl.BlockSpec((1, rows, qw), cur_map),
            pl.BlockSpec((1, nprev, kvw), prev_map),
            pl.BlockSpec((1, rows, kvw), cur_map),
            pl.BlockSpec((1, nprev, kvw), prev_map),
            pl.BlockSpec((1, rows, kvw), cur_map),
            pl.BlockSpec((1, rows, qw), cur_map),
            pl.BlockSpec((kvh, rep * cq, span), lambda bi, i: (0, 0, 0)),
        ],
        out_specs=pl.BlockSpec((1, rows, qw), cur_map),
        out_shape=jax.ShapeDtypeStruct((b, sq, qw), BF16),
        compiler_params=_cparams(("parallel", "arbitrary")),
        name=name,
    )(sinks.astype(F32), q, k_prev_src, k_cur, v_prev_src, v_cur, g, _alibi_bias(kvh, rep, cq, span, nprev))


def _ssm_kernel(u_ref, w_ref, m_ref, v_ref, lam_ref, h0_ref, y_ref, hfin_ref, s_scr, h_scr, carry, *, nchains):
    ti = pl.program_id(2)
    nrow = s_scr.shape[0]
    half = s_scr.shape[1] // 2
    steps = nrow // nchains

    @pl.when(ti == 0)
    def _():
        carry[...] = h0_ref[0, 0]

    u = u_ref[0, 0]
    s_scr[...] = jnp.dot(u, w_ref[0], preferred_element_type=F32)

    lam = lam_ref[0]
    lr = jnp.broadcast_to(lam[:, :half], (nchains, half))
    li = jnp.broadcast_to(lam[:, half:], (nchains, half))

    def step(c, h):
        r0 = pl.multiple_of(c * nchains, nchains)
        h_scr[pl.ds(r0, nchains), :] = h
        s = s_scr[pl.ds(r0, nchains), :]
        hr, hi = h[:, :half], h[:, half:]
        nr = lr * hr - li * hi + s[:, :half]
        ni = lr * hi + li * hr + s[:, half:]
        return jnp.concatenate([nr, ni], axis=1)

    h = lax.fori_loop(0, steps, step, carry[...], unroll=(8 if steps % 8 == 0 else 1))
    carry[...] = h
    hfin_ref[0, 0] = h

    y_ref[0, 0] = (jnp.dot(u, m_ref[0], preferred_element_type=F32)
                   + jnp.dot(h_scr[...].astype(BF16), v_ref[0], preferred_element_type=F32))


def _ssm_tables(a_re, a_im, log_dt, b_re, b_im, c_re, c_im, L):
    g_, p_ = a_re.shape
    sg = SSM_SLAB_GROUPS
    ns = g_ // sg
    hp = lax.Precision.HIGHEST
    a_re = jnp.minimum(a_re.astype(F32), -1e-4)
    a_im = a_im.astype(F32)
    dt = jnp.exp(log_dt.astype(F32))[:, None]
    mag = jnp.exp(dt * a_re)
    lb_re = mag * jnp.cos(dt * a_im)
    lb_im = mag * jnp.sin(dt * a_im)
    n_re = lb_re - 1.0
    den = a_re * a_re + a_im * a_im
    f_re = ((n_re * a_re + lb_im * a_im) / den)[..., None]
    f_im = ((lb_im * a_re - n_re * a_im) / den)[..., None]
    br, bi = b_re.astype(F32), b_im.astype(F32)
    bb_re = f_re * br - f_im * bi
    bb_im = f_re * bi + f_im * br
    cr, ci = c_re.astype(F32), c_im.astype(F32)
    tau = jnp.arange(L + 1, dtype=F32)[:, None, None]
    pmag = jnp.exp(tau * dt * a_re)
    pw_re = pmag * jnp.cos(tau * dt * a_im)
    pw_im = pmag * jnp.sin(tau * dt * a_im)
    eye = jnp.eye(sg, dtype=F32)

    lb_r = pw_re[:, :, :, None] * bb_re[None] - pw_im[:, :, :, None] * bb_im[None]
    lb_i = pw_re[:, :, :, None] * bb_im[None] + pw_im[:, :, :, None] * bb_re[None]
    kt = (jnp.einsum('gop,tgpi->tgoi', cr, lb_r[:L], precision=hp)
          - jnp.einsum('gop,tgpi->tgoi', ci, lb_i[:L], precision=hp))
    lag = jnp.arange(L)[None, :] - jnp.arange(L)[:, None]
    kst = jnp.where((lag >= 0)[:, :, None, None, None], kt[jnp.clip(lag, 0, L - 1)], 0.0)
    kst = kst.reshape(L, L, ns, sg, SSM_GROUP, SSM_GROUP)
    m = jnp.einsum('stnhoi,hk->nshitko', kst, eye).reshape(ns, L * LANES, L * LANES)
    wri = jnp.stack([lb_r[:L][::-1], lb_i[:L][::-1]], axis=0).reshape(2, L, ns, sg, p_, SSM_GROUP)
    w = jnp.einsum('asnhpi,hk->nshiakp', wri, eye).reshape(ns, L * LANES, 2 * sg * p_)
    cl_r = cr[None] * pw_re[1:][:, :, None, :] - ci[None] * pw_im[1:][:, :, None, :]
    cl_i = cr[None] * pw_im[1:][:, :, None, :] + ci[None] * pw_re[1:][:, :, None, :]
    vri = jnp.stack([cl_r, -cl_i], axis=0).reshape(2, L, ns, sg, SSM_GROUP, p_)
    v = jnp.einsum('atnhop,hk->nahptko', vri, eye).reshape(ns, 2 * sg * p_, L * LANES)
    lam = jnp.stack([pw_re[L], pw_im[L]], axis=0).reshape(2, ns, sg * p_)
    lam = jnp.transpose(lam, (1, 0, 2)).reshape(ns, 1, 2 * sg * p_)
    return w.astype(BF16), m.astype(BF16), v.astype(BF16), lam


def _ssm(u_rows, h0, tables, *, nchains, tile_rows, name):
    w, m, v, lam = tables
    ns, b, r, lw = u_rows.shape
    sw = w.shape[2]
    tile_rows = min(tile_rows, r)
    assert r % tile_rows == 0 and tile_rows % nchains == 0
    row_map = lambda o, bi, ti: (o, bi, ti, 0)
    tab = lambda o, bi, ti: (o, 0, 0)
    st_map = lambda o, bi, ti: (bi, o, 0, 0)
    return pl.pallas_call(
        functools.partial(_ssm_kernel, nchains=nchains),
        grid=(ns, b, r // tile_rows),
        in_specs=[
            pl.BlockSpec((1, 1, tile_rows, lw), row_map),
            pl.BlockSpec((1, lw, sw), tab),
            pl.BlockSpec((1, lw, lw), tab),
            pl.BlockSpec((1, sw, lw), tab),
            pl.BlockSpec((1, 1, sw), tab),
            pl.BlockSpec((1, 1, nchains, sw), st_map),
        ],
        out_specs=[
            pl.BlockSpec((1, 1, tile_rows, lw), row_map),
            pl.BlockSpec((1, 1, nchains, sw), st_map),
        ],
        out_shape=[jax.ShapeDtypeStruct((ns, b, r, lw), F32),
                   jax.ShapeDtypeStruct((b, ns, nchains, sw), F32)],
        scratch_shapes=[pltpu.VMEM((tile_rows, sw), F32), pltpu.VMEM((tile_rows, sw), F32),
                        pltpu.VMEM((nchains, sw), F32)],
        compiler_params=_cparams(("parallel", "parallel", "arbitrary")),
        name=name,
    )(u_rows, w, m, v, lam, h0)


def _gelu_tanh(x):
    return 0.5 * x * (1.0 + jnp.tanh(math.sqrt(2.0 / math.pi) * (x + 0.044715 * (x * x * x))))


def _s5_tail_kernel(y_ref, u_ref, g_ref, d_ref, wg_ref, wo_ref, x_ref, lg_ref, lb_ref, o_ref, *, alpha):
    ns = y_ref.shape[0]
    y = jnp.concatenate([y_ref[o] for o in range(ns)], axis=1)
    u = jnp.concatenate([u_ref[o] for o in range(ns)], axis=1).astype(F32)
    y = _gelu_tanh(y + d_ref[...] * u)
    z = jnp.dot(y.astype(BF16), wg_ref[...], preferred_element_type=F32)
    y = y * (1.0 / (1.0 + jnp.exp(-z)))
    a = (y * _silu(g_ref[...].astype(F32))).astype(BF16)
    out = jnp.dot(a, wo_ref[...], preferred_element_type=F32)
    r = alpha * x_ref[...] + out
    o_ref[...] = _layer_norm_rows(r, lg_ref[...], lb_ref[...])


def _s5_tail(y_slab, u_slab, g2d, d_vec, w_glu, w_out, x2d, ln_g, ln_b, alpha, tm, name):
    ns, t, _ = y_slab.shape
    c = ns * LANES
    d = x2d.shape[1]
    tm = min(tm, t)
    assert t % tm == 0
    slab_spec = pl.BlockSpec((ns, tm, LANES), lambda i: (0, i, 0))
    const = lambda i: (0, 0)
    return pl.pallas_call(
        functools.partial(_s5_tail_kernel, alpha=alpha),
        grid=(t // tm,),
        in_specs=[
            slab_spec, slab_spec,
            pl.BlockSpec((tm, c), lambda i: (i, 0)),
            pl.BlockSpec((1, c), const),
            pl.BlockSpec((c, c), const, pipeline_mode=pl.Buffered(1)),
            pl.BlockSpec((c, d), const, pipeline_mode=pl.Buffered(1)),
            pl.BlockSpec((tm, d), lambda i: (i, 0)),
            pl.BlockSpec((1, d), const),
            pl.BlockSpec((1, d), const),
        ],
        out_specs=pl.BlockSpec((tm, d), lambda i: (i, 0)),
        out_shape=jax.ShapeDtypeStruct((t, d), F32),
        compiler_params=_cparams(("parallel",)),
        name=name,
    )(y_slab, u_slab, g2d, d_vec.reshape(1, c).astype(F32), w_glu, w_out, x2d,
      ln_g.reshape(1, d).astype(F32), ln_b.reshape(1, d).astype(F32))


TM_PROJ = 512
TM_OUT = 256
SB_CHAINS = 4
SWA_CHUNKS_PER_STEP = 4
SSM_TILE_ROWS = 256


def _sb_layer(xp, xs, bp, bs, cache_k, cache_v, w_in, w_out, ln_g, ln_b, alpha):
    heads, hd = cache_k.shape[2], cache_k.shape[3]
    width = heads * hd
    past = cache_k.shape[1]
    sp, ss = xp.shape[0] // bp, xs.shape[0] // bs
    w_in = w_in.astype(BF16)
    w_out = w_out.astype(BF16)
    segs = ((0, width, hd ** -0.5, (BF16,), False),
            (width, width, 1.0, (BF16, F32), False),
            (2 * width, width, 1.0, (BF16, F32), False),
            (3 * width, width, 1.0, (BF16,), False))
    q, kb, kf, vb, vf, g = _proj(xp, w_in, segs, TM_PROJ, "sb_proj_prompt")
    r3 = lambda a, b_, s_: a.reshape(b_, s_, width)
    og = _sb_attention(r3(q, bp, sp), r3(kb, bp, sp), r3(vb, bp, sp), r3(g, bp, sp), heads,
                       qb=SB_KBLOCK, nchain=SB_CHAINS, q_pos0=0, name="sb_attn_prompt")
    xp_new = _outproj_ln(og.reshape(bp * sp, width), w_out, xp, ln_g, ln_b, alpha, TM_OUT, "sb_out_prompt")
    new_p = (kf.reshape(bp, sp, heads, hd), vf.reshape(bp, sp, heads, hd))
    q, kb, kf, vb, vf, g = _proj(xs, w_in, segs, TM_PROJ, "sb_proj_sample")
    n_pad = -(-(past + ss) // SB_KBLOCK) * SB_KBLOCK
    pad = jnp.zeros((bs, n_pad - past - ss, width), BF16)
    keys = jnp.concatenate([cache_k.reshape(bs, past, width).astype(BF16), r3(kb, bs, ss), pad], axis=1)
    vals = jnp.concatenate([cache_v.reshape(bs, past, width).astype(BF16), r3(vb, bs, ss), pad], axis=1)
    og = _sb_attention(r3(q, bs, ss), keys, vals, r3(g, bs, ss), heads,
                       qb=ss, nchain=1, q_pos0=past, name="sb_attn_sample")
    xs_new = _outproj_ln(og.reshape(bs * ss, width), w_out, xs, ln_g, ln_b, alpha, TM_OUT, "sb_out_sample")
    new_s = (kf.reshape(bs, ss, heads, hd), vf.reshape(bs, ss, heads, hd))
    return xp_new, xs_new, new_p, new_s


def _swa_layer(xp, xs, bp, bs, cache_k, cache_v, w_in, sinks, w_out, ln_g, ln_b, alpha):
    kvh, hd = cache_k.shape[2], cache_k.shape[3]
    rep = sinks.shape[1]
    qw, kvw = kvh * rep * hd, kvh * hd
    nprev = cache_k.shape[1]
    sp, ss = xp.shape[0] // bp, xs.shape[0] // bs
    assert nprev == WINDOW and sp >= WINDOW
    w_in = w_in.astype(BF16)
    w_out = w_out.astype(BF16)
    segs = ((0, qw, hd ** -0.5, (BF16,), False),
            (qw, kvw, 1.0, (BF16, F32), False),
            (qw + kvw, kvw, 1.0, (BF16, F32), False),
            (qw + 2 * kvw, qw, 1.0, (BF16,), False))
    q, kb, kf, vb, vf, g = _proj(xp, w_in, segs, TM_PROJ, "swa_proj_prompt")
    k3, v3 = kb.reshape(bp, sp, kvw), vb.reshape(bp, sp, kvw)
    og = _swa_attention(q.reshape(bp, sp, qw), k3, k3, v3, v3, g.reshape(bp, sp, qw), sinks,
                        cq=CHUNK, nch=SWA_CHUNKS_PER_STEP, nprev=WINDOW, prev_from_cur=True,
                        name="swa_attn_prompt")
    xp_new = _outproj_ln(og.reshape(bp * sp, qw), w_out, xp, ln_g, ln_b, alpha, TM_OUT, "swa_out_prompt")
    new_p = (kf.reshape(bp, sp, kvh, hd)[:, sp - WINDOW:], vf.reshape(bp, sp, kvh, hd)[:, sp - WINDOW:])
    q, kb, kf, vb, vf, g = _proj(xs, w_in, segs, TM_PROJ, "swa_proj_sample")
    og = _swa_attention(q.reshape(bs, ss, qw),
                        cache_k.reshape(bs, nprev, kvw).astype(BF16), kb.reshape(bs, ss, kvw),
                        cache_v.reshape(bs, nprev, kvw).astype(BF16), vb.reshape(bs, ss, kvw),
                        g.reshape(bs, ss, qw), sinks,
                        cq=ss, nch=1, nprev=nprev, prev_from_cur=False, name="swa_attn_sample")
    xs_new = _outproj_ln(og.reshape(bs * ss, qw), w_out, xs, ln_g, ln_b, alpha, TM_OUT, "swa_out_sample")
    new_s = (kf.reshape(bs, ss, kvh, hd), vf.reshape(bs, ss, kvh, hd))
    return xp_new, xs_new, new_p, new_s


def _s5_layer(xp, xs, bp, bs, state_re, state_im, w_in, a_re, a_im, log_dt, b_re, b_im, c_re, c_im,
              d_vec, w_glu, w_out, ln_g, ln_b, alpha):
    groups, p_ = a_re.shape
    c = groups * SSM_GROUP
    ns = c // LANES
    sg = SSM_SLAB_GROUPS
    L = SSM_L
    sp, ss = xp.shape[0] // bp, xs.shape[0] // bs
    assert sp % L == 0 and ss % L == 0
    w_in = w_in.astype(BF16)
    w_glu = w_glu.astype(BF16)
    w_out = w_out.astype(BF16)
    tables = _ssm_tables(a_re, a_im, log_dt, b_re, b_im, c_re, c_im, L)
    segs = ((0, c, 1.0, (BF16,), True), (c, c, 1.0, (BF16,), False))

    def split_state(hfin):
        re = hfin[..., :sg * p_].reshape(hfin.shape[:-1] + (sg, p_))
        im = hfin[..., sg * p_:].reshape(hfin.shape[:-1] + (sg, p_))
        return re, im

    u, g = _proj(xp, w_in, segs, TM_PROJ, "s5_proj_prompt")
    h0 = jnp.zeros((bp, ns, 1, 2 * sg * p_), F32)
    y, hfin = _ssm(u.reshape(ns, bp, sp // L, L * LANES), h0, tables,
                   nchains=1, tile_rows=SSM_TILE_ROWS, name="s5_ssm_prompt")
    xp_new = _s5_tail(y.reshape(ns, bp * sp, LANES), u, g, d_vec, w_glu, w_out, xp, ln_g, ln_b, alpha,
                      TM_OUT, "s5_tail_prompt")
    re, im = split_state(hfin[:, :, 0])
    new_p = (re.reshape(bp, groups, p_), im.reshape(bp, groups, p_))

    u, g = _proj(xs, w_in, segs, TM_PROJ, "s5_proj_sample")
    nck = ss // L
    u_rows = u.reshape(ns, bs, nck, L * LANES).transpose(0, 2, 1, 3).reshape(ns, 1, nck * bs, L * LANES)
    st = jnp.concatenate([state_re.astype(F32).reshape(bs, ns, sg * p_),
                          state_im.astype(F32).reshape(bs, ns, sg * p_)], axis=-1)
    h0 = st.transpose(1, 0, 2)[None]
    y, hfin = _ssm(u_rows, h0, tables, nchains=bs, tile_rows=nck * bs, name="s5_ssm_sample")
    y = y.reshape(ns, nck, bs, L * LANES).transpose(0, 2, 1, 3).reshape(ns, bs * ss, LANES)
    xs_new = _s5_tail(y, u, g, d_vec, w_glu, w_out, xs, ln_g, ln_b, alpha, TM_OUT, "s5_tail_sample")
    re, im = split_state(hfin[0].transpose(1, 0, 2))
    new_s = (re.reshape(bs, groups, p_), im.reshape(bs, groups, p_))
    return xp_new, xs_new, new_p, new_s


def kernel(x_prompt, x_sample, cache_k_a, cache_v_a, cache_k_b, cache_v_b, state_re_c, state_im_c, ln_g, ln_b, w_in_a, w_out_a, w_in_b, sinks_b, w_out_b, w_in_c, a_re_c, a_im_c, log_dt_c, b_re_c, b_im_c, c_re_c, c_im_c, d_c, w_glu_c, w_out_c):
    depth = ln_g.shape[0]
    alpha = (2 * depth) ** 0.25
    bp, sp, d = x_prompt.shape
    bs, ss, _ = x_sample.shape
    xp = x_prompt.reshape(bp * sp, d)
    xs = x_sample.reshape(bs * ss, d)
    outs = {k: [] for k in ("ka_p", "va_p", "ka_s", "va_s", "kb_p", "vb_p", "kb_s", "vb_s",
                            "hr_p", "hi_p", "hr_s", "hi_s")}
    for i in range(depth):
        j, kind = i // N_MIXERS, i % N_MIXERS
        if kind == 0:
            xp, xs, new_p, new_s = _sb_layer(xp, xs, bp, bs, cache_k_a[j], cache_v_a[j], w_in_a[j], w_out_a[j],
                                             ln_g[i], ln_b[i], alpha)
            names = ("ka_p", "va_p", "ka_s", "va_s")
        elif kind == 1:
            xp, xs, new_p, new_s = _swa_layer(xp, xs, bp, bs, cache_k_b[j], cache_v_b[j], w_in_b[j], sinks_b[j],
                                              w_out_b[j], ln_g[i], ln_b[i], alpha)
            names = ("kb_p", "vb_p", "kb_s", "vb_s")
        else:
            xp, xs, new_p, new_s = _s5_layer(xp, xs, bp, bs, state_re_c[j], state_im_c[j], w_in_c[j], a_re_c[j],
                                             a_im_c[j], log_dt_c[j], b_re_c[j], b_im_c[j], c_re_c[j], c_im_c[j],
                                             d_c[j], w_glu_c[j], w_out_c[j], ln_g[i], ln_b[i], alpha)
            names = ("hr_p", "hi_p", "hr_s", "hi_s")
        for nm, val in zip(names, new_p + new_s):
            outs[nm].append(val)
    return (xp.reshape(bp, sp, d), xs.reshape(bs, ss, d),
            jnp.stack(outs["ka_p"]), jnp.stack(outs["va_p"]), jnp.stack(outs["ka_s"]), jnp.stack(outs["va_s"]),
            jnp.stack(outs["kb_p"]), jnp.stack(outs["vb_p"]), jnp.stack(outs["kb_s"]), jnp.stack(outs["vb_s"]),
            jnp.stack(outs["hr_p"]), jnp.stack(outs["hi_p"]), jnp.stack(outs["hr_s"]), jnp.stack(outs["hi_s"]))
```

```python
import functools
import math

import jax
import jax.numpy as jnp
from jax import lax
from jax.experimental import pallas as pl
from jax.experimental.pallas import tpu as pltpu

F32 = jnp.float32
BF16 = jnp.bfloat16

CHUNK = 64
WINDOW = 128
N_MIXERS = 3
SSM_GROUP = 16
LN_EPS = 1e-5
SB_KBLOCK = 128

LANES = 128
SUBLANES = 8
VMEM_LIMIT = 56 * 1024 * 1024

LOG2E = math.log2(math.e)
SB_DEAD_LOG2 = -151.0
SSM_L = SUBLANES
SSM_SLAB_GROUPS = LANES // SSM_GROUP


def _silu(x):
    return x * (1.0 / (1.0 + jnp.exp(-x)))


def _cparams(sem, vmem=VMEM_LIMIT):
    return pltpu.CompilerParams(dimension_semantics=sem, vmem_limit_bytes=vmem)


def _proj_kernel(*refs, segs, n_alias, step, stack_layer):
    x_ref, w_ref = refs[0], refs[1]
    out_refs = refs[2 + n_alias:]
    xb = x_ref[...].astype(BF16)
    outs = iter(out_refs)
    scr = out_refs[-1]
    for c0, width, scale, kinds in segs:
        o_refs = [next(outs) for _ in kinds]
        for s0 in range(0, width, step):
            s1 = min(width, s0 + step)
            acc = jnp.dot(xb, w_ref[:, c0 + s0:c0 + s1], preferred_element_type=F32)
            if scale != 1.0:
                acc = acc * scale
            for (kind, _), o_ref in zip(kinds, o_refs):
                if kind == "rows":
                    rows = acc.shape[0] // SUBLANES
                    for l0 in range(s0, s1, LANES):
                        slab = (l0 - s0) // LANES
                        scr[slab] = acc[:, l0 - s0:l0 - s0 + LANES]
                        for s in range(SUBLANES):
                            piece = scr[slab, pl.ds(s, rows, stride=SUBLANES), :]
                            o_ref[l0 // LANES, :, s * LANES:(s + 1) * LANES] = piece.astype(o_ref.dtype)
                elif kind == "stack_init":
                    for layer in range(o_ref.shape[0]):
                        val = acc.astype(o_ref.dtype) if layer == stack_layer else jnp.zeros(acc.shape, o_ref.dtype)
                        o_ref[layer, :, s0:s1] = val
                else:
                    o_ref[:, s0:s1] = acc.astype(o_ref.dtype)


def _proj(x2d, w_bf16, segs, tm, name, stack=None):
    t, d = x2d.shape
    n = w_bf16.shape[1]
    tm = min(tm, t)
    assert t % tm == 0 and tm % SUBLANES == 0
    step = 1024
    out_specs, out_shape, alias_in = [], [], []
    need_scratch = False
    for _, wd, _, kinds in segs:
        for kind, dt in kinds:
            if kind == "rows":
                need_scratch = True
                out_specs.append(pl.BlockSpec((wd // LANES, tm // SUBLANES, SUBLANES * LANES), lambda i: (0, i, 0)))
                out_shape.append(jax.ShapeDtypeStruct((wd // LANES, t // SUBLANES, SUBLANES * LANES), dt))
            elif kind == "stack_init":
                out_specs.append(pl.BlockSpec((stack[1], tm, wd), lambda i: (0, i, 0)))
                out_shape.append(jax.ShapeDtypeStruct((stack[1], t, wd), dt))
            elif kind == "stack":
                j, nl, prev = stack
                out_specs.append(pl.BlockSpec((None, tm, wd), functools.partial(lambda i, j_: (j_, i, 0), j_=j)))
                out_shape.append(jax.ShapeDtypeStruct((nl, t, wd), dt))
                alias_in.append((len(out_shape) - 1, prev[len(alias_in)]))
            else:
                out_specs.append(pl.BlockSpec((tm, wd), lambda i: (i, 0)))
                out_shape.append(jax.ShapeDtypeStruct((t, wd), dt))
    in_specs = [
        pl.BlockSpec((tm, d), lambda i: (i, 0)),
        pl.BlockSpec((d, n), lambda i: (0, 0), pipeline_mode=pl.Buffered(1)),
    ] + [pl.BlockSpec(memory_space=pl.ANY) for _ in alias_in]
    scratch = [pltpu.VMEM((step // LANES, tm, LANES) if need_scratch else (1, SUBLANES, LANES), F32)]
    kern = functools.partial(_proj_kernel, segs=segs, n_alias=len(alias_in), step=step,
                             stack_layer=None if stack is None else stack[0])
    return pl.pallas_call(
        kern,
        grid=(t // tm,),
        in_specs=in_specs,
        out_specs=out_specs,
        out_shape=out_shape,
        scratch_shapes=scratch,
        input_output_aliases={2 + k: oi for k, (oi, _) in enumerate(alias_in)},
        compiler_params=_cparams(("parallel",)),
        name=name,
    )(x2d, w_bf16, *[buf for _, buf in alias_in])


def _layer_norm_rows(r, g, b):
    mu = jnp.mean(r, axis=-1, keepdims=True)
    rc = r - mu
    var = jnp.mean(rc * rc, axis=-1, keepdims=True)
    return rc * lax.rsqrt(var + LN_EPS) * g + b


def _outproj_kernel(a_ref, w_ref, x_ref, g_ref, b_ref, o_ref, *, alpha):
    y = jnp.dot(a_ref[...], w_ref[...], preferred_element_type=F32)
    r = alpha * x_ref[...] + y
    o_ref[...] = _layer_norm_rows(r, g_ref[...], b_ref[...])


def _outproj_ln(a2d, w_bf16, x2d, ln_g, ln_b, alpha, tm, name):
    t, wdt = a2d.shape
    d = x2d.shape[1]
    tm = min(tm, t)
    assert t % tm == 0
    return pl.pallas_call(
        functools.partial(_outproj_kernel, alpha=alpha),
        grid=(t // tm,),
        in_specs=[
            pl.BlockSpec((tm, wdt), lambda i: (i, 0)),
            pl.BlockSpec((wdt, d), lambda i: (0, 0), pipeline_mode=pl.Buffered(1)),
            pl.BlockSpec((tm, d), lambda i: (i, 0)),
            pl.BlockSpec((1, d), lambda i: (0, 0)),
            pl.BlockSpec((1, d), lambda i: (0, 0)),
        ],
        out_specs=pl.BlockSpec((tm, d), lambda i: (i, 0)),
        out_shape=jax.ShapeDtypeStruct((t, d), F32),
        compiler_params=_cparams(("parallel",)),
        name=name,
    )(a2d, w_bf16, x2d, ln_g.reshape(1, d).astype(F32), ln_b.reshape(1, d).astype(F32))


def _scores(qs, kblks):
    return [lax.dot_general(q, k, (((1,), (1,)), ((), ())), preferred_element_type=F32) for q, k in zip(qs, kblks)]


def _sb_blocks(zs, vblks, masks, laters, accs, tri):
    n = len(zs)
    rows = zs[0].shape[0]
    lbetas, cats = [], []
    for c in range(n):
        z = zs[c]
        sp = jnp.maximum(z, 0.0) + jnp.log2(1.0 + jnp.exp2(-jnp.abs(z)))
        lbetas.append(z - sp)
        if masks[c] is not None:
            sp = jnp.where(masks[c], sp, 0.0)
        hi = sp.astype(BF16)
        lo = (sp - hi.astype(F32)).astype(BF16)
        cats.append(jnp.concatenate([hi, lo], axis=1))
    cs = jnp.dot(jnp.concatenate(cats, axis=0), tri, preferred_element_type=F32)
    new_l, ws = [], []
    for c in range(n):
        blk = cs[c * rows:(c + 1) * rows]
        w = jnp.exp2(lbetas[c] + blk[:, :SB_KBLOCK] + laters[c])
        if masks[c] is not None:
            w = jnp.where(masks[c], w, 0.0)
        ws.append(w.astype(BF16))
        new_l.append(laters[c] + blk[:, SB_KBLOCK:])
    new_a = [accs[c] + jnp.dot(ws[c], vblks[c], preferred_element_type=F32) for c in range(n)]
    return new_l, new_a


def _sb_alive(ls):
    m = ls[0]
    for l in ls[1:]:
        m = jnp.maximum(m, l)
    return jnp.max(m) > SB_DEAD_LOG2


def _sb_tri():
    j = jnp.arange(SB_KBLOCK)
    later = -(j[:, None] > j[None, :]).astype(BF16)
    ones = -jnp.ones((SB_KBLOCK, SB_KBLOCK), BF16)
    one = jnp.concatenate([later, ones], axis=1)
    return jnp.concatenate([one, one], axis=0)


def _sb_prompt_kernel(q_ref, k_ref, v_ref, g_ref, tri_ref, o_ref, *, nchain):
    kb = SB_KBLOCK
    i = pl.program_id(2)
    n0 = [i * nchain + c for c in range(nchain)]
    n0_max = n0[-1]
    col = lax.broadcasted_iota(jnp.int32, (kb, kb), 1)
    row = lax.broadcasted_iota(jnp.int32, (kb, kb), 0)

    def offs(nbs):
        return [pl.multiple_of(jnp.maximum(nb, 0) * kb, kb) for nb in nbs]

    def loadk(nbs):
        return [k_ref[0, pl.ds(o, kb), :] for o in offs(nbs)]

    def loadv(nbs):
        return [v_ref[0, pl.ds(o, kb), :] for o in offs(nbs)]

    def queries():
        return [q_ref[0, c * kb:(c + 1) * kb, :] for c in range(nchain)]

    zero = jnp.zeros((kb, kb), F32)
    diag = col < row
    laters, accs = _sb_blocks(_scores(queries(), loadk(n0)), loadv(n0), [diag] * nchain,
                              [zero] * nchain, [zero] * nchain, tri_ref[...])

    def cond(st):
        return jnp.logical_and(st[0] <= n0_max, st[1])

    def body(st):
        t = st[0]
        nbs = [n0[c] - t for c in range(nchain)]
        ls = [st[2][c] + jnp.where(nbs[c] >= 0, 0.0, -1e30).astype(F32) for c in range(nchain)]
        ls, as_ = _sb_blocks(_scores(queries(), loadk(nbs)), loadv(nbs), [None] * nchain, ls, list(st[3]),
                             tri_ref[...])
        return (t + 1, _sb_alive(ls), tuple(ls), tuple(as_))

    st = lax.while_loop(cond, body, (jnp.int32(1), _sb_alive(laters), tuple(laters), tuple(accs)))
    accs = st[3]
    for c in range(nchain):
        g = g_ref[0, c * kb:(c + 1) * kb, :].astype(F32)
        o_ref[0, c * kb:(c + 1) * kb, :] = (accs[c] * _silu(g)).astype(o_ref.dtype)


def _sb_attention_prompt(q, k, v, g, heads, nchain, name):
    b, s, w = q.shape
    hd = w // heads
    tq = SB_KBLOCK * nchain
    assert hd == SB_KBLOCK and s % tq == 0
    qmap = lambda bi, h, i: (bi, i, h)
    kmap = lambda bi, h, i: (bi, 0, h)
    return pl.pallas_call(
        functools.partial(_sb_prompt_kernel, nchain=nchain),
        grid=(b, heads, s // tq),
        in_specs=[
            pl.BlockSpec((1, tq, hd), qmap),
            pl.BlockSpec((1, s, hd), kmap),
            pl.BlockSpec((1, s, hd), kmap),
            pl.BlockSpec((1, tq, hd), qmap),
            pl.BlockSpec((2 * SB_KBLOCK, 2 * SB_KBLOCK), lambda bi, h, i: (0, 0)),
        ],
        out_specs=pl.BlockSpec((1, tq, hd), qmap),
        out_shape=jax.ShapeDtypeStruct((b, s, w), BF16),
        compiler_params=_cparams(("parallel", "parallel", "arbitrary")),
        name=name,
    )(q, k, v, g, _sb_tri())


def _sb_sample_kernel(q_ref, kn_ref, vn_ref, g_ref, ck_ref, cv_ref, ckh_ref, cvh_ref, tri_ref, o_ref,
                      kscr, vscr, sem, *, heads, wblk, nrem):
    kb = SB_KBLOCK
    b = pl.program_id(0)
    ss = q_ref.shape[1]
    qs = [q_ref[0, :, h * kb:(h + 1) * kb] for h in range(heads)]
    zpad = jnp.zeros((kb - ss, kb), BF16)
    col = lax.broadcasted_iota(jnp.int32, (ss, kb), 1)
    row = lax.broadcasted_iota(jnp.int32, (ss, kb), 0)
    new_mask = col < row
    kbl = [jnp.concatenate([kn_ref[0, :, h * kb:(h + 1) * kb], zpad], axis=0) for h in range(heads)]
    vbl = [jnp.concatenate([vn_ref[0, :, h * kb:(h + 1) * kb], zpad], axis=0) for h in range(heads)]
    zero = jnp.zeros((ss, kb), F32)
    laters, accs = _sb_blocks(_scores(qs, kbl), vbl, [new_mask] * heads, [zero] * heads, [zero] * heads, tri_ref[...])

    def head_blocks(ref, r0):
        return [ref[pl.ds(r0 + h, kb, stride=heads), :].astype(BF16) for h in range(heads)]

    for j in range(wblk):
        r0 = (wblk - 1 - j) * kb * heads
        laters, accs = _sb_blocks(_scores(qs, head_blocks(ck_ref.at[0], r0)), head_blocks(cv_ref.at[0], r0),
                                  [None] * heads, laters, accs, tri_ref[...])

    if nrem > 0:
        def cond(st):
            return jnp.logical_and(st[0] < nrem, st[1])

        def body(st):
            t = st[0]
            r0 = pl.multiple_of((nrem - 1 - t) * (kb * heads), kb * heads)
            ck = pltpu.make_async_copy(ckh_ref.at[b, pl.ds(r0, kb * heads)], kscr, sem.at[0])
            cv = pltpu.make_async_copy(cvh_ref.at[b, pl.ds(r0, kb * heads)], vscr, sem.at[1])
            ck.start()
            cv.start()
            ck.wait()
            cv.wait()
            ls, as_ = _sb_blocks(_scores(qs, head_blocks(kscr, 0)), head_blocks(vscr, 0), [None] * heads,
                                 list(st[2]), list(st[3]), tri_ref[...])
            return (t + 1, _sb_alive(ls), tuple(ls), tuple(as_))

        st = lax.while_loop(cond, body, (jnp.int32(0), _sb_alive(laters), tuple(laters), tuple(accs)))
        accs = st[3]
    for h in range(heads):
        g = g_ref[0, :, h * kb:(h + 1) * kb].astype(F32)
        o_ref[0, :, h * kb:(h + 1) * kb] = (accs[h] * _silu(g)).astype(o_ref.dtype)


def _sb_attention_sample(q, k_new, v_new, g, cache_k, cache_v, wblk, name):
    b, ss, w = q.shape
    _, past, heads, hd = cache_k.shape
    assert hd == SB_KBLOCK and w == heads * hd and ss <= SB_KBLOCK and ss % 16 == 0
    wblk = min(wblk, past // SB_KBLOCK)
    tail = wblk * SB_KBLOCK
    assert past % tail == 0
    nrem = past // SB_KBLOCK - wblk
    ck = cache_k.reshape(b, past * heads, hd)
    cv = cache_v.reshape(b, past * heads, hd)
    tok = lambda bi: (bi, 0, 0)
    tail_map = lambda bi: (bi, past // tail - 1, 0)
    kern = functools.partial(_sb_sample_kernel, heads=heads, wblk=wblk, nrem=nrem)
    return pl.pallas_call(
        kern,
        grid=(b,),
        in_specs=[
            pl.BlockSpec((1, ss, w), tok),
            pl.BlockSpec((1, ss, w), tok),
            pl.BlockSpec((1, ss, w), tok),
            pl.BlockSpec((1, ss, w), tok),
            pl.BlockSpec((1, tail * heads, hd), tail_map),
            pl.BlockSpec((1, tail * heads, hd), tail_map),
            pl.BlockSpec(memory_space=pl.ANY),
            pl.BlockSpec(memory_space=pl.ANY),
            pl.BlockSpec((2 * SB_KBLOCK, 2 * SB_KBLOCK), lambda bi: (0, 0)),
        ],
        out_specs=pl.BlockSpec((1, ss, w), tok),
        out_shape=jax.ShapeDtypeStruct((b, ss, w), BF16),
        scratch_shapes=[pltpu.VMEM((SB_KBLOCK * heads, hd), F32), pltpu.VMEM((SB_KBLOCK * heads, hd), F32),
                        pltpu.SemaphoreType.DMA((2,))],
        compiler_params=_cparams(("arbitrary",)),
        name=name,
    )(q, k_new, v_new, g, ck, cv, ck, cv, _sb_tri())


def _swa_kernel(q_ref, kp_ref, kc_ref, vp_ref, vc_ref, g_ref, bias_ref, sink_ref, o_ref,
                *, cq, nch, span, kvh, rep, hd, mask_first):
    i = pl.program_id(1)
    kbuf = jnp.concatenate([kp_ref[0], kc_ref[0]], axis=0)
    vbuf = jnp.concatenate([vp_ref[0], vc_ref[0]], axis=0)
    nprev = kp_ref.shape[1]
    gw = rep * hd
    for kv in range(kvh):
        kk = kbuf[:, kv * hd:(kv + 1) * hd]
        vv = vbuf[:, kv * hd:(kv + 1) * hd]
        bias = bias_ref[kv]
        sink = sink_ref[kv]
        for j in range(nch):
            ks = kk[j * cq:j * cq + span]
            vs = vv[j * cq:j * cq + span]
            qs = q_ref[0, j * cq:(j + 1) * cq, kv * gw:(kv + 1) * gw]
            lhs = jnp.concatenate([qs[:, r * hd:(r + 1) * hd] for r in range(rep)], axis=0)
            st = lax.dot_general(ks, lhs, (((1,), (1,)), ((), ())), preferred_element_type=F32) + bias
            if mask_first and j * cq < nprev:
                buf_row = j * cq + lax.broadcasted_iota(jnp.int32, st.shape, 0)
                valid = jnp.logical_or(i > 0, buf_row >= nprev)
                st = jnp.where(valid, st, -jnp.inf)
            m = jnp.maximum(jnp.max(st, axis=0, keepdims=True), sink)
            p = jnp.exp2(st - m)
            den = jnp.sum(p, axis=0, keepdims=True) + jnp.exp2(sink - m)
            p = (p * (1.0 / den)).astype(BF16)
            o = lax.dot_general(p, vs, (((0,), (0,)), ((), ())), preferred_element_type=F32)
            ot = jnp.concatenate([o[r * cq:(r + 1) * cq] for r in range(rep)], axis=1)
            g = g_ref[0, j * cq:(j + 1) * cq, kv * gw:(kv + 1) * gw].astype(F32)
            o_ref[0, j * cq:(j + 1) * cq, kv * gw:(kv + 1) * gw] = (ot * _silu(g)).astype(o_ref.dtype)


def _alibi_bias_t(kvh, rep, cq, span, nprev):
    n = kvh * rep
    slopes = 2.0 ** (-8.0 * jnp.arange(1, n + 1, dtype=F32) / n)
    qi = jnp.arange(cq, dtype=jnp.int32)
    kj = jnp.arange(span, dtype=jnp.int32)
    dist = jnp.abs(qi[None, :] + nprev - kj[:, None]).astype(F32)
    bias = -(slopes * LOG2E).reshape(kvh, 1, rep, 1) * dist[None, :, None, :]
    return bias.reshape(kvh, span, rep * cq)


def _swa_attention(q, k_prev_src, k_cur, v_prev_src, v_cur, g, sinks, *, cq, nch, nprev, prev_from_cur, name):
    b, sq, qw = q.shape
    kvw = k_cur.shape[2]
    kvh, rep = sinks.shape
    hd = kvw // kvh
    rows = cq * nch
    span = nprev + cq
    assert sq % rows == 0 and qw == kvh * rep * hd
    if prev_from_cur:
        assert rows % nprev == 0
        ratio = rows // nprev
        prev_map = lambda bi, i: (bi, jnp.maximum(i * ratio - 1, 0), 0)
    else:
        assert sq == rows
        prev_map = lambda bi, i: (bi, 0, 0)
    cur_map = lambda bi, i: (bi, i, 0)
    const3 = lambda bi, i: (0, 0, 0)
    sink_t = jnp.broadcast_to((sinks.astype(F32) * LOG2E)[:, None, :, None], (kvh, 1, rep, cq)).reshape(kvh, 1, rep * cq)
    kern = functools.partial(_swa_kernel, cq=cq, nch=nch, span=span, kvh=kvh, rep=rep, hd=hd,
                             mask_first=prev_from_cur)
    return pl.pallas_call(
        kern,
        grid=(b, sq // rows),
        in_specs=[
            pl.BlockSpec((1, rows, qw), cur_map),
            pl.BlockSpec((1, nprev, kvw), prev_map),
            pl.BlockSpec((1, rows, kvw), cur_map),
            pl.BlockSpec((1, nprev, kvw), prev_map),
            pl.BlockSpec((1, rows, kvw), cur_map),
            pl.BlockSpec((1, rows, qw), cur_map),
            pl.BlockSpec((kvh, span, rep * cq), const3),
            pl.BlockSpec((kvh, 1, rep * cq), const3),
        ],
        out_specs=pl.BlockSpec((1, rows, qw), cur_map),
        out_shape=jax.ShapeDtypeStruct((b, sq, qw), BF16),
        compiler_params=_cparams(("parallel", "arbitrary")),
        name=name,
    )(q, k_prev_src, k_cur, v_prev_src, v_cur, g, _alibi_bias_t(kvh, rep, cq, span, nprev), sink_t)


def _ssm_kernel(u_ref, w_ref, m_ref, v_ref, lam_ref, h0_ref, y_ref, hfin_ref, s_scr, h_scr, carry, *, nchains):
    ti = pl.program_id(2)
    nrow = s_scr.shape[0]
    half = s_scr.shape[1] // 2
    steps = nrow // nchains

    @pl.when(ti == 0)
    def _():
        carry[...] = h0_ref[0, 0]

    u = u_ref[0, 0]
    s_scr[...] = jnp.dot(u, w_ref[0], preferred_element_type=F32)

    lam = lam_ref[0]
    lr = jnp.broadcast_to(lam[:, :half], (nchains, half))
    li = jnp.broadcast_to(lam[:, half:], (nchains, half))

    def step(c, h):
        r0 = pl.multiple_of(c * nchains, nchains)
        h_scr[pl.ds(r0, nchains), :] = h
        s = s_scr[pl.ds(r0, nchains), :]
        hr, hi = h[:, :half], h[:, half:]
        nr = lr * hr - li * hi + s[:, :half]
        ni = lr * hi + li * hr + s[:, half:]
        return jnp.concatenate([nr, ni], axis=1)

    h = lax.fori_loop(0, steps, step, carry[...], unroll=(8 if steps % 8 == 0 else 1))
    carry[...] = h
    hfin_ref[0, 0] = h

    y_ref[0, 0] = (jnp.dot(u, m_ref[0], preferred_element_type=F32)
                   + jnp.dot(h_scr[...].astype(BF16), v_ref[0], preferred_element_type=F32))


def _ssm_tables(a_re, a_im, log_dt, b_re, b_im, c_re, c_im, L):
    g_, p_ = a_re.shape
    sg = SSM_SLAB_GROUPS
    ns = g_ // sg
    hp = lax.Precision.HIGHEST
    a_re = jnp.minimum(a_re.astype(F32), -1e-4)
    a_im = a_im.astype(F32)
    dt = jnp.exp(log_dt.astype(F32))[:, None]
    mag = jnp.exp(dt * a_re)
    lb_re = mag * jnp.cos(dt * a_im)
    lb_im = mag * jnp.sin(dt * a_im)
    n_re = lb_re - 1.0
    den = a_re * a_re + a_im * a_im
    f_re = ((n_re * a_re + lb_im * a_im) / den)[..., None]
    f_im = ((lb_im * a_re - n_re * a_im) / den)[..., None]
    br, bi = b_re.astype(F32), b_im.astype(F32)
    bb_re = f_re * br - f_im * bi
    bb_im = f_re * bi + f_im * br
    cr, ci = c_re.astype(F32), c_im.astype(F32)
    tau = jnp.arange(L + 1, dtype=F32)[:, None, None]
    pmag = jnp.exp(tau * dt * a_re)
    pw_re = pmag * jnp.cos(tau * dt * a_im)
    pw_im = pmag * jnp.sin(tau * dt * a_im)

    lb_r = pw_re[:, :, :, None] * bb_re[None] - pw_im[:, :, :, None] * bb_im[None]
    lb_i = pw_re[:, :, :, None] * bb_im[None] + pw_im[:, :, :, None] * bb_re[None]
    kt = (jnp.einsum('gop,tgpi->tgoi', cr, lb_r[:L], precision=hp)
          - jnp.einsum('gop,tgpi->tgoi', ci, lb_i[:L], precision=hp))
    lag = jnp.arange(L)[None, :] - jnp.arange(L)[:, None]
    kst = jnp.where((lag >= 0)[:, :, None, None, None], kt[jnp.clip(lag, 0, L - 1)], 0.0)

    rows_u = L * LANES
    cols_s = 2 * sg * p_
    rid = jnp.arange(rows_u)
    grp_u = (rid % LANES) // SSM_GROUP
    sid = jnp.arange(cols_s)
    grp_s = (sid % (sg * p_)) // p_

    def expand(compact, rep_mat, grp_r, grp_c):
        full = jnp.einsum('nrc,ck->nrk', compact, rep_mat, precision=hp)
        return jnp.where(grp_r[:, None] == grp_c[None, :], full, 0.0).astype(BF16)

    m_c = kst.reshape(L, L, ns, sg, SSM_GROUP, SSM_GROUP).transpose(2, 0, 3, 5, 1, 4).reshape(ns, rows_u, L * SSM_GROUP)
    cid = jnp.arange(rows_u)
    rep_to = ((cid[None, :] // LANES == jnp.arange(L * SSM_GROUP)[:, None] // SSM_GROUP)
              & (cid[None, :] % SSM_GROUP == jnp.arange(L * SSM_GROUP)[:, None] % SSM_GROUP)).astype(F32)
    m = expand(m_c, rep_to, grp_u, grp_u)
    wri = jnp.stack([lb_r[:L][::-1], lb_i[:L][::-1]], axis=0).reshape(2, L, ns, sg, p_, SSM_GROUP)
    w_c = wri.transpose(2, 1, 3, 5, 0, 4).reshape(ns, rows_u, 2 * p_)
    rep_ap = ((sid[None, :] // (sg * p_) == jnp.arange(2 * p_)[:, None] // p_)
              & (sid[None, :] % p_ == jnp.arange(2 * p_)[:, None] % p_)).astype(F32)
    w = expand(w_c, rep_ap, grp_u, grp_s)
    cl_r = cr[None] * pw_re[1:][:, :, None, :] - ci[None] * pw_im[1:][:, :, None, :]
    cl_i = cr[None] * pw_im[1:][:, :, None, :] + ci[None] * pw_re[1:][:, :, None, :]
    vri = jnp.stack([cl_r, -cl_i], axis=0).reshape(2, L, ns, sg, SSM_GROUP, p_)
    v_c = vri.transpose(2, 0, 3, 5, 1, 4).reshape(ns, cols_s, L * SSM_GROUP)
    v = expand(v_c, rep_to, grp_s, grp_u)
    lam = jnp.stack([pw_re[L], pw_im[L]], axis=0).reshape(2, ns, sg * p_)
    lam = jnp.transpose(lam, (1, 0, 2)).reshape(ns, 1, cols_s)
    return w, m, v, lam


def _ssm(u_rows, h0, tables, *, nchains, tile_rows, name):
    w, m, v, lam = tables
    ns, b, r, lw = u_rows.shape
    sw = w.shape[2]
    tile_rows = min(tile_rows, r)
    assert r % tile_rows == 0 and tile_rows % nchains == 0
    row_map = lambda o, bi, ti: (o, bi, ti, 0)
    tab = lambda o, bi, ti: (o, 0, 0)
    st_map = lambda o, bi, ti: (bi, o, 0, 0)
    return pl.pallas_call(
        functools.partial(_ssm_kernel, nchains=nchains),
        grid=(ns, b, r // tile_rows),
        in_specs=[
            pl.BlockSpec((1, 1, tile_rows, lw), row_map),
            pl.BlockSpec((1, lw, sw), tab),
            pl.BlockSpec((1, lw, lw), tab),
            pl.BlockSpec((1, sw, lw), tab),
            pl.BlockSpec((1, 1, sw), tab),
            pl.BlockSpec((1, 1, nchains, sw), st_map),
        ],
        out_specs=[
            pl.BlockSpec((1, 1, tile_rows, lw), row_map),
            pl.BlockSpec((1, 1, nchains, sw), st_map),
        ],
        out_shape=[jax.ShapeDtypeStruct((ns, b, r, lw), F32),
                   jax.ShapeDtypeStruct((b, ns, nchains, sw), F32)],
        scratch_shapes=[pltpu.VMEM((tile_rows, sw), F32), pltpu.VMEM((tile_rows, sw), F32),
                        pltpu.VMEM((nchains, sw), F32)],
        compiler_params=_cparams(("parallel", "parallel", "arbitrary")),
        name=name,
    )(u_rows, w, m, v, lam, h0)


def _gelu_tanh(x):
    return 0.5 * x * (1.0 + jnp.tanh(math.sqrt(2.0 / math.pi) * (x + 0.044715 * (x * x * x))))


def _s5_tail_kernel(y_ref, u_ref, g_ref, d_ref, wg_ref, wo_ref, x_ref, lg_ref, lb_ref, o_ref, y_scr, u_scr, *, alpha):
    ns, rows, _ = y_ref.shape
    for o in range(ns):
        for s in range(SUBLANES):
            y_scr[o, pl.ds(s, rows, stride=SUBLANES), :] = y_ref[o, :, s * LANES:(s + 1) * LANES]
            u_scr[o, pl.ds(s, rows, stride=SUBLANES), :] = u_ref[o, :, s * LANES:(s + 1) * LANES].astype(F32)
    y_tok = jnp.concatenate([y_scr[o] for o in range(ns)], axis=1)
    u_tok = jnp.concatenate([u_scr[o] for o in range(ns)], axis=1)
    y = _gelu_tanh(y_tok + d_ref[...] * u_tok)
    z = jnp.dot(y.astype(BF16), wg_ref[...], preferred_element_type=F32)
    y = y * (1.0 / (1.0 + jnp.exp(-z)))
    a = (y * _silu(g_ref[...].astype(F32))).astype(BF16)
    out = jnp.dot(a, wo_ref[...], preferred_element_type=F32)
    r = alpha * x_ref[...] + out
    o_ref[...] = _layer_norm_rows(r, lg_ref[...], lb_ref[...])


def _s5_tail(y_rows, u_rows, g2d, d_vec, w_glu, w_out, x2d, ln_g, ln_b, alpha, tm, name):
    ns, tr, lw = y_rows.shape
    t = tr * SUBLANES
    c = ns * LANES
    d = x2d.shape[1]
    tm = min(tm, t)
    assert t % tm == 0 and tm % SUBLANES == 0 and lw == SUBLANES * LANES
    row_spec = pl.BlockSpec((ns, tm // SUBLANES, lw), lambda i: (0, i, 0))
    const = lambda i: (0, 0)
    return pl.pallas_call(
        functools.partial(_s5_tail_kernel, alpha=alpha),
        grid=(t // tm,),
        in_specs=[
            row_spec, row_spec,
            pl.BlockSpec((tm, c), lambda i: (i, 0)),
            pl.BlockSpec((1, c), const),
            pl.BlockSpec((c, c), const, pipeline_mode=pl.Buffered(1)),
            pl.BlockSpec((c, d), const, pipeline_mode=pl.Buffered(1)),
            pl.BlockSpec((tm, d), lambda i: (i, 0)),
            pl.BlockSpec((1, d), const),
            pl.BlockSpec((1, d), const),
        ],
        out_specs=pl.BlockSpec((tm, d), lambda i: (i, 0)),
        out_shape=jax.ShapeDtypeStruct((t, d), F32),
        scratch_shapes=[pltpu.VMEM((ns, tm, LANES), F32), pltpu.VMEM((ns, tm, LANES), F32)],
        compiler_params=_cparams(("parallel",)),
        name=name,
    )(y_rows, u_rows, g2d, d_vec.reshape(1, c).astype(F32), w_glu, w_out, x2d,
      ln_g.reshape(1, d).astype(F32), ln_b.reshape(1, d).astype(F32))


TM_PROJ = 512
TM_OUT = 256
SB_CHAINS = 8
SB_SAMPLE_TAIL_BLOCKS = 2
SWA_CHUNKS_PER_STEP = 4
SSM_TILE_ROWS = 256


def _sb_layer(xp, xs, bp, bs, cache_k, cache_v, w_in, w_out, ln_g, ln_b, alpha, stack_p):
    heads, hd = cache_k.shape[2], cache_k.shape[3]
    width = heads * hd
    sp, ss = xp.shape[0] // bp, xs.shape[0] // bs
    w_in = w_in.astype(BF16)
    w_out = w_out.astype(BF16)

    def segs(kv_kind):
        return ((0, width, LOG2E * hd ** -0.5, (("plain", BF16),)),
                (width, width, 1.0, (("plain", BF16), (kv_kind, F32))),
                (2 * width, width, 1.0, (("plain", BF16), (kv_kind, F32))),
                (3 * width, width, 1.0, (("plain", BF16),)))

    r3 = lambda a, b_, s_: a.reshape(b_, s_, width)
    kv_kind = "stack_init" if stack_p[2] is None else "stack"
    q, kb, kf, vb, vf, g = _proj(xp, w_in, segs(kv_kind), TM_PROJ, "sb_proj_prompt", stack=stack_p)
    og = _sb_attention_prompt(r3(q, bp, sp), r3(kb, bp, sp), r3(vb, bp, sp), r3(g, bp, sp), heads,
                              nchain=min(SB_CHAINS, sp // SB_KBLOCK), name="sb_attn_prompt")
    xp_new = _outproj_ln(og.reshape(bp * sp, width), w_out, xp, ln_g, ln_b, alpha, TM_OUT, "sb_out_prompt")
    q, kb, kf_s, vb, vf_s, g = _proj(xs, w_in, segs("plain"), TM_PROJ, "sb_proj_sample")
    og = _sb_attention_sample(r3(q, bs, ss), r3(kb, bs, ss), r3(vb, bs, ss), r3(g, bs, ss), cache_k, cache_v,
                              SB_SAMPLE_TAIL_BLOCKS, "sb_attn_sample")
    xs_new = _outproj_ln(og.reshape(bs * ss, width), w_out, xs, ln_g, ln_b, alpha, TM_OUT, "sb_out_sample")
    new_s = (kf_s.reshape(bs, ss, heads, hd), vf_s.reshape(bs, ss, heads, hd))
    return xp_new, xs_new, (kf, vf), new_s


def _swa_layer(xp, xs, bp, bs, cache_k, cache_v, w_in, sinks, w_out, ln_g, ln_b, alpha):
    kvh, hd = cache_k.shape[2], cache_k.shape[3]
    rep = sinks.shape[1]
    qw, kvw = kvh * rep * hd, kvh * hd
    nprev = cache_k.shape[1]
    sp, ss = xp.shape[0] // bp, xs.shape[0] // bs
    assert nprev == WINDOW and sp >= WINDOW
    w_in = w_in.astype(BF16)
    w_out = w_out.astype(BF16)
    segs = ((0, qw, LOG2E * hd ** -0.5, (("plain", BF16),)),
            (qw, kvw, 1.0, (("plain", BF16), ("plain", F32))),
            (qw + kvw, kvw, 1.0, (("plain", BF16), ("plain", F32))),
            (qw + 2 * kvw, qw, 1.0, (("plain", BF16),)))
    q, kb, kf, vb, vf, g = _proj(xp, w_in, segs, TM_PROJ, "swa_proj_prompt")
    k3, v3 = kb.reshape(bp, sp, kvw), vb.reshape(bp, sp, kvw)
    og = _swa_attention(q.reshape(bp, sp, qw), k3, k3, v3, v3, g.reshape(bp, sp, qw), sinks,
                        cq=CHUNK, nch=SWA_CHUNKS_PER_STEP, nprev=WINDOW, prev_from_cur=True,
                        name="swa_attn_prompt")
    xp_new = _outproj_ln(og.reshape(bp * sp, qw), w_out, xp, ln_g, ln_b, alpha, TM_OUT, "swa_out_prompt")
    last = lambda a: a.reshape(bp, sp, kvw)[:, sp - WINDOW:].reshape(bp, WINDOW, kvh, hd)
    new_p = (last(kf), last(vf))
    q, kb, kf, vb, vf, g = _proj(xs, w_in, segs, TM_PROJ, "swa_proj_sample")
    og = _swa_attention(q.reshape(bs, ss, qw),
                        cache_k.reshape(bs, nprev, kvw).astype(BF16), kb.reshape(bs, ss, kvw),
                        cache_v.reshape(bs, nprev, kvw).astype(BF16), vb.reshape(bs, ss, kvw),
                        g.reshape(bs, ss, qw), sinks,
                        cq=ss, nch=1, nprev=nprev, prev_from_cur=False, name="swa_attn_sample")
    xs_new = _outproj_ln(og.reshape(bs * ss, qw), w_out, xs, ln_g, ln_b, alpha, TM_OUT, "swa_out_sample")
    new_s = (kf.reshape(bs, ss, kvh, hd), vf.reshape(bs, ss, kvh, hd))
    return xp_new, xs_new, new_p, new_s


def _s5_layer(xp, xs, bp, bs, state_re, state_im, w_in, a_re, a_im, log_dt, b_re, b_im, c_re, c_im,
              d_vec, w_glu, w_out, ln_g, ln_b, alpha):
    groups, p_ = a_re.shape
    c = groups * SSM_GROUP
    ns = c // LANES
    sg = SSM_SLAB_GROUPS
    L = SSM_L
    lw = L * LANES
    sp, ss = xp.shape[0] // bp, xs.shape[0] // bs
    assert sp % L == 0 and ss % L == 0
    w_in = w_in.astype(BF16)
    w_glu = w_glu.astype(BF16)
    w_out = w_out.astype(BF16)
    tables = _ssm_tables(a_re, a_im, log_dt, b_re, b_im, c_re, c_im, L)
    segs = ((0, c, 1.0, (("rows", BF16),)), (c, c, 1.0, (("plain", BF16),)))

    def split_state(hfin):
        re = hfin[..., :sg * p_].reshape(hfin.shape[:-1] + (sg, p_))
        im = hfin[..., sg * p_:].reshape(hfin.shape[:-1] + (sg, p_))
        return re, im

    u, g = _proj(xp, w_in, segs, TM_PROJ, "s5_proj_prompt")
    h0 = jnp.zeros((bp, ns, 1, 2 * sg * p_), F32)
    y, hfin = _ssm(u.reshape(ns, bp, sp // L, lw), h0, tables,
                   nchains=1, tile_rows=SSM_TILE_ROWS, name="s5_ssm_prompt")
    xp_new = _s5_tail(y.reshape(ns, bp * sp // L, lw), u, g, d_vec, w_glu, w_out, xp, ln_g, ln_b, alpha,
                      TM_OUT, "s5_tail_prompt")
    re, im = split_state(hfin[:, :, 0])
    new_p = (re.reshape(bp, groups, p_), im.reshape(bp, groups, p_))

    u, g = _proj(xs, w_in, segs, TM_PROJ, "s5_proj_sample")
    nck = ss // L
    u_rows = u.reshape(ns, bs, nck, lw).transpose(0, 2, 1, 3).reshape(ns, 1, nck * bs, lw)
    st = jnp.concatenate([state_re.astype(F32).reshape(bs, ns, sg * p_),
                          state_im.astype(F32).reshape(bs, ns, sg * p_)], axis=-1)
    h0 = st.transpose(1, 0, 2)[None]
    y, hfin = _ssm(u_rows, h0, tables, nchains=bs, tile_rows=nck * bs, name="s5_ssm_sample")
    y = y.reshape(ns, nck, bs, lw).transpose(0, 2, 1, 3).reshape(ns, bs * nck, lw)
    xs_new = _s5_tail(y, u, g, d_vec, w_glu, w_out, xs, ln_g, ln_b, alpha, TM_OUT, "s5_tail_sample")
    re, im = split_state(hfin[0].transpose(1, 0, 2))
    new_s = (re.reshape(bs, groups, p_), im.reshape(bs, groups, p_))
    return xp_new, xs_new, new_p, new_s


def kernel(x_prompt, x_sample, cache_k_a, cache_v_a, cache_k_b, cache_v_b, state_re_c, state_im_c, ln_g, ln_b, w_in_a, w_out_a, w_in_b, sinks_b, w_out_b, w_in_c, a_re_c, a_im_c, log_dt_c, b_re_c, b_im_c, c_re_c, c_im_c, d_c, w_glu_c, w_out_c):
    depth = ln_g.shape[0]
    alpha = (2 * depth) ** 0.25
    bp, sp, d = x_prompt.shape
    bs, ss, _ = x_sample.shape
    xp = x_prompt.reshape(bp * sp, d)
    xs = x_sample.reshape(bs * ss, d)
    n_a = cache_k_a.shape[0]
    heads_a, hd_a = cache_k_a.shape[3], cache_k_a.shape[4]
    outs = {k: [] for k in ("ka_s", "va_s", "kb_p", "vb_p", "kb_s", "vb_s", "hr_p", "hi_p", "hr_s", "hi_s")}
    sb_stack = None
    for i in range(depth):
        j, kind = i // N_MIXERS, i % N_MIXERS
        if kind == 0:
            xp, xs, sb_stack, new_s = _sb_layer(xp, xs, bp, bs, cache_k_a[j], cache_v_a[j], w_in_a[j], w_out_a[j],
                                                ln_g[i], ln_b[i], alpha, (j, n_a, sb_stack))
            new_p, names = (), ("ka_s", "va_s")
        elif kind == 1:
            xp, xs, new_p, new_s = _swa_layer(xp, xs, bp, bs, cache_k_b[j], cache_v_b[j], w_in_b[j], sinks_b[j],
                                              w_out_b[j], ln_g[i], ln_b[i], alpha)
            names = ("kb_p", "vb_p", "kb_s", "vb_s")
        else:
            xp, xs, new_p, new_s = _s5_layer(xp, xs, bp, bs, state_re_c[j], state_im_c[j], w_in_c[j], a_re_c[j],
                                             a_im_c[j], log_dt_c[j], b_re_c[j], b_im_c[j], c_re_c[j], c_im_c[j],
                                             d_c[j], w_glu_c[j], w_out_c[j], ln_g[i], ln_b[i], alpha)
            names = ("hr_p", "hi_p", "hr_s", "hi_s")
        for nm, val in zip(names, tuple(new_p) + tuple(new_s)):
            outs[nm].append(val)
    ka_p = sb_stack[0].reshape(n_a, bp, sp, heads_a, hd_a)
    va_p = sb_stack[1].reshape(n_a, bp, sp, heads_a, hd_a)
    return (xp.reshape(bp, sp, d), xs.reshape(bs, ss, d),
            ka_p, va_p, jnp.stack(outs["ka_s"]), jnp.stack(outs["va_s"]),
            jnp.stack(outs["kb_p"]), jnp.stack(outs["vb_p"]), jnp.stack(outs["kb_s"]), jnp.stack(outs["vb_s"]),
            jnp.stack(outs["hr_p"]), jnp.stack(outs["hi_p"]), jnp.stack(outs["hr_s"]), jnp.stack(outs["hi_s"]))
```

```python
import functools
import math

import jax
import jax.numpy as jnp
from jax import lax
from jax.experimental import pallas as pl
from jax.experimental.pallas import tpu as pltpu

F32 = jnp.float32
BF16 = jnp.bfloat16

CHUNK = 64
WINDOW = 128
N_MIXERS = 3
SSM_GROUP = 16
LN_EPS = 1e-5
SB_KBLOCK = 128

LANES = 128
SUBLANES = 8
VMEM_LIMIT = 56 * 1024 * 1024

LOG2E = math.log2(math.e)
SB_DEAD_LOG2 = -127.0
SSM_L = SUBLANES
SSM_SLAB_GROUPS = LANES // SSM_GROUP


def _silu(x):
    return x * (1.0 / (1.0 + jnp.exp(-x)))


def _cparams(sem, vmem=VMEM_LIMIT):
    return pltpu.CompilerParams(dimension_semantics=sem, vmem_limit_bytes=vmem)


def _proj_kernel(*refs, segs, n_alias, step, stack_layer):
    x_ref, w_ref = refs[0], refs[1]
    out_refs = refs[2 + n_alias:]
    xb = x_ref[...].astype(BF16)
    outs = iter(out_refs)
    scr = out_refs[-1]
    for c0, width, scale, kinds in segs:
        o_refs = [next(outs) for _ in kinds]
        for s0 in range(0, width, step):
            s1 = min(width, s0 + step)
            acc = jnp.dot(xb, w_ref[:, c0 + s0:c0 + s1], preferred_element_type=F32)
            if scale != 1.0:
                acc = acc * scale
            for (kind, _), o_ref in zip(kinds, o_refs):
                if kind == "rows":
                    rows = acc.shape[0] // SUBLANES
                    for l0 in range(s0, s1, LANES):
                        slab = (l0 - s0) // LANES
                        scr[slab] = acc[:, l0 - s0:l0 - s0 + LANES]
                        for s in range(SUBLANES):
                            piece = scr[slab, pl.ds(s, rows, stride=SUBLANES), :]
                            o_ref[l0 // LANES, :, s * LANES:(s + 1) * LANES] = piece.astype(o_ref.dtype)
                elif kind == "stack_init":
                    for layer in range(o_ref.shape[0]):
                        val = acc.astype(o_ref.dtype) if layer == stack_layer else jnp.zeros(acc.shape, o_ref.dtype)
                        o_ref[layer, :, s0:s1] = val
                else:
                    o_ref[:, s0:s1] = acc.astype(o_ref.dtype)


def _proj(x2d, w_bf16, segs, tm, name, stack=None):
    t, d = x2d.shape
    n = w_bf16.shape[1]
    tm = min(tm, t)
    assert t % tm == 0 and tm % SUBLANES == 0
    step = 1024
    out_specs, out_shape, alias_in = [], [], []
    need_scratch = False
    for _, wd, _, kinds in segs:
        for kind, dt in kinds:
            if kind == "rows":
                need_scratch = True
                out_specs.append(pl.BlockSpec((wd // LANES, tm // SUBLANES, SUBLANES * LANES), lambda i: (0, i, 0)))
                out_shape.append(jax.ShapeDtypeStruct((wd // LANES, t // SUBLANES, SUBLANES * LANES), dt))
            elif kind == "stack_init":
                out_specs.append(pl.BlockSpec((stack[1], tm, wd), lambda i: (0, i, 0)))
                out_shape.append(jax.ShapeDtypeStruct((stack[1], t, wd), dt))
            elif kind == "stack":
                j, nl, prev = stack
                out_specs.append(pl.BlockSpec((None, tm, wd), functools.partial(lambda i, j_: (j_, i, 0), j_=j)))
                out_shape.append(jax.ShapeDtypeStruct((nl, t, wd), dt))
                alias_in.append((len(out_shape) - 1, prev[len(alias_in)]))
            else:
                out_specs.append(pl.BlockSpec((tm, wd), lambda i: (i, 0)))
                out_shape.append(jax.ShapeDtypeStruct((t, wd), dt))
    in_specs = [
        pl.BlockSpec((tm, d), lambda i: (i, 0)),
        pl.BlockSpec((d, n), lambda i: (0, 0), pipeline_mode=pl.Buffered(1)),
    ] + [pl.BlockSpec(memory_space=pl.ANY) for _ in alias_in]
    scratch = [pltpu.VMEM((step // LANES, tm, LANES) if need_scratch else (1, SUBLANES, LANES), F32)]
    kern = functools.partial(_proj_kernel, segs=segs, n_alias=len(alias_in), step=step,
                             stack_layer=None if stack is None else stack[0])
    return pl.pallas_call(
        kern,
        grid=(t // tm,),
        in_specs=in_specs,
        out_specs=out_specs,
        out_shape=out_shape,
        scratch_shapes=scratch,
        input_output_aliases={2 + k: oi for k, (oi, _) in enumerate(alias_in)},
        compiler_params=_cparams(("parallel",)),
        name=name,
    )(x2d, w_bf16, *[buf for _, buf in alias_in])


def _layer_norm_rows(r, g, b):
    mu = jnp.mean(r, axis=-1, keepdims=True)
    rc = r - mu
    var = jnp.mean(rc * rc, axis=-1, keepdims=True)
    return rc * lax.rsqrt(var + LN_EPS) * g + b


def _outproj_kernel(a_ref, w_ref, x_ref, g_ref, b_ref, o_ref, *, alpha, sub):
    for r0 in range(0, a_ref.shape[0], sub):
        y = jnp.dot(a_ref[r0:r0 + sub, :], w_ref[...], preferred_element_type=F32)
        r = alpha * x_ref[r0:r0 + sub, :] + y
        o_ref[r0:r0 + sub, :] = _layer_norm_rows(r, g_ref[...], b_ref[...])


def _outproj_ln(a2d, w_bf16, x2d, ln_g, ln_b, alpha, tm, name):
    t, wdt = a2d.shape
    d = x2d.shape[1]
    tm = min(tm, t)
    assert t % tm == 0
    return pl.pallas_call(
        functools.partial(_outproj_kernel, alpha=alpha, sub=min(tm, OUT_SUB_ROWS)),
        grid=(t // tm,),
        in_specs=[
            pl.BlockSpec((tm, wdt), lambda i: (i, 0)),
            pl.BlockSpec((wdt, d), lambda i: (0, 0), pipeline_mode=pl.Buffered(1)),
            pl.BlockSpec((tm, d), lambda i: (i, 0)),
            pl.BlockSpec((1, d), lambda i: (0, 0)),
            pl.BlockSpec((1, d), lambda i: (0, 0)),
        ],
        out_specs=pl.BlockSpec((tm, d), lambda i: (i, 0)),
        out_shape=jax.ShapeDtypeStruct((t, d), F32),
        compiler_params=_cparams(("parallel",)),
        name=name,
    )(a2d, w_bf16, x2d, ln_g.reshape(1, d).astype(F32), ln_b.reshape(1, d).astype(F32))


def _scores(qs, kblks):
    return [lax.dot_general(q, k, (((1,), (1,)), ((), ())), preferred_element_type=F32) for q, k in zip(qs, kblks)]


def _sb_blocks(zs, vblks, masks, laters, accs, tri):
    n = len(zs)
    rows = zs[0].shape[0]
    lbetas, cats = [], []
    for c in range(n):
        z = zs[c]
        sp = jnp.maximum(z, 0.0) + jnp.log2(1.0 + jnp.exp2(-jnp.abs(z)))
        lbetas.append(z - sp)
        if masks[c] is not None:
            sp = jnp.where(masks[c], sp, 0.0)
        hi = sp.astype(BF16)
        lo = (sp - hi.astype(F32)).astype(BF16)
        cats.append(jnp.concatenate([hi, lo], axis=1))
    cs = jnp.dot(jnp.concatenate(cats, axis=0), tri, preferred_element_type=F32)
    new_l, ws = [], []
    for c in range(n):
        blk = cs[c * rows:(c + 1) * rows]
        w = jnp.exp2(lbetas[c] + blk[:, :SB_KBLOCK] + laters[c])
        if masks[c] is not None:
            w = jnp.where(masks[c], w, 0.0)
        ws.append(w.astype(BF16))
        new_l.append(laters[c] + blk[:, SB_KBLOCK:])
    new_a = [accs[c] + jnp.dot(ws[c], vblks[c], preferred_element_type=F32) for c in range(n)]
    return new_l, new_a


def _sb_alive(ls):
    m = ls[0]
    for l in ls[1:]:
        m = jnp.maximum(m, l)
    return jnp.max(m) > SB_DEAD_LOG2


def _sb_tri():
    j = jnp.arange(SB_KBLOCK)
    later = -(j[:, None] > j[None, :]).astype(BF16)
    ones = -jnp.ones((SB_KBLOCK, SB_KBLOCK), BF16)
    one = jnp.concatenate([later, ones], axis=1)
    return jnp.concatenate([one, one], axis=0)


def _sb_prompt_kernel(q_ref, k_ref, v_ref, g_ref, tri_ref, o_ref, *, nchain):
    kb = SB_KBLOCK
    i = pl.program_id(2)
    n0 = [i * nchain + c for c in range(nchain)]
    n0_max = n0[-1]
    col = lax.broadcasted_iota(jnp.int32, (kb, kb), 1)
    row = lax.broadcasted_iota(jnp.int32, (kb, kb), 0)

    def offs(nbs):
        return [pl.multiple_of(jnp.maximum(nb, 0) * kb, kb) for nb in nbs]

    def loadk(nbs):
        return [k_ref[0, pl.ds(o, kb), :] for o in offs(nbs)]

    def loadv(nbs):
        return [v_ref[0, pl.ds(o, kb), :] for o in offs(nbs)]

    def queries():
        return [q_ref[0, c * kb:(c + 1) * kb, :] for c in range(nchain)]

    zero = jnp.zeros((kb, kb), F32)
    diag = col < row
    laters, accs = _sb_blocks(_scores(queries(), loadk(n0)), loadv(n0), [diag] * nchain,
                              [zero] * nchain, [zero] * nchain, tri_ref[...])

    def cond(st):
        return jnp.logical_and(st[0] <= n0_max, st[1])

    def body(st):
        t = st[0]
        nbs = [n0[c] - t for c in range(nchain)]
        ls = [st[2][c] + jnp.where(nbs[c] >= 0, 0.0, -1e30).astype(F32) for c in range(nchain)]
        ls, as_ = _sb_blocks(_scores(queries(), loadk(nbs)), loadv(nbs), [None] * nchain, ls, list(st[3]),
                             tri_ref[...])
        return (t + 1, _sb_alive(ls), tuple(ls), tuple(as_))

    st = lax.while_loop(cond, body, (jnp.int32(1), _sb_alive(laters), tuple(laters), tuple(accs)))
    accs = st[3]
    for c in range(nchain):
        g = g_ref[0, c * kb:(c + 1) * kb, :].astype(F32)
        o_ref[0, c * kb:(c + 1) * kb, :] = (accs[c] * _silu(g)).astype(o_ref.dtype)


def _sb_attention_prompt(q, k, v, g, heads, nchain, name):
    b, s, w = q.shape
    hd = w // heads
    tq = SB_KBLOCK * nchain
    assert hd == SB_KBLOCK and s % tq == 0
    qmap = lambda bi, h, i: (bi, i, h)
    kmap = lambda bi, h, i: (bi, 0, h)
    return pl.pallas_call(
        functools.partial(_sb_prompt_kernel, nchain=nchain),
        grid=(b, heads, s // tq),
        in_specs=[
            pl.BlockSpec((1, tq, hd), qmap),
            pl.BlockSpec((1, s, hd), kmap),
            pl.BlockSpec((1, s, hd), kmap),
            pl.BlockSpec((1, tq, hd), qmap),
            pl.BlockSpec((2 * SB_KBLOCK, 2 * SB_KBLOCK), lambda bi, h, i: (0, 0)),
        ],
        out_specs=pl.BlockSpec((1, tq, hd), qmap),
        out_shape=jax.ShapeDtypeStruct((b, s, w), BF16),
        compiler_params=_cparams(("parallel", "parallel", "arbitrary")),
        name=name,
    )(q, k, v, g, _sb_tri())


def _sb_sample_kernel(q_ref, kn_ref, vn_ref, g_ref, ck_ref, cv_ref, ckh_ref, cvh_ref, tri_ref, o_ref,
                      kscr, vscr, sem, *, heads, wblk, nrem, layer):
    kb = SB_KBLOCK
    b = pl.program_id(0)
    ss = q_ref.shape[1]
    qs = [q_ref[0, :, h * kb:(h + 1) * kb] for h in range(heads)]
    zpad = jnp.zeros((kb - ss, kb), BF16)
    col = lax.broadcasted_iota(jnp.int32, (ss, kb), 1)
    row = lax.broadcasted_iota(jnp.int32, (ss, kb), 0)
    new_mask = col < row
    kbl = [jnp.concatenate([kn_ref[0, :, h * kb:(h + 1) * kb], zpad], axis=0) for h in range(heads)]
    vbl = [jnp.concatenate([vn_ref[0, :, h * kb:(h + 1) * kb], zpad], axis=0) for h in range(heads)]
    zero = jnp.zeros((ss, kb), F32)
    laters, accs = _sb_blocks(_scores(qs, kbl), vbl, [new_mask] * heads, [zero] * heads, [zero] * heads, tri_ref[...])

    def head_blocks(ref, r0):
        return [ref[pl.ds(r0 + h, kb, stride=heads), :].astype(BF16) for h in range(heads)]

    for j in range(wblk):
        r0 = (wblk - 1 - j) * kb * heads
        laters, accs = _sb_blocks(_scores(qs, head_blocks(ck_ref.at[0], r0)), head_blocks(cv_ref.at[0], r0),
                                  [None] * heads, laters, accs, tri_ref[...])

    if nrem > 0:
        def cond(st):
            return jnp.logical_and(st[0] < nrem, st[1])

        def body(st):
            t = st[0]
            r0 = pl.multiple_of((nrem - 1 - t) * (kb * heads), kb * heads)
            ck = pltpu.make_async_copy(ckh_ref.at[layer, b, pl.ds(r0, kb * heads)], kscr, sem.at[0])
            cv = pltpu.make_async_copy(cvh_ref.at[layer, b, pl.ds(r0, kb * heads)], vscr, sem.at[1])
            ck.start()
            cv.start()
            ck.wait()
            cv.wait()
            ls, as_ = _sb_blocks(_scores(qs, head_blocks(kscr, 0)), head_blocks(vscr, 0), [None] * heads,
                                 list(st[2]), list(st[3]), tri_ref[...])
            return (t + 1, _sb_alive(ls), tuple(ls), tuple(as_))

        st = lax.while_loop(cond, body, (jnp.int32(0), _sb_alive(laters), tuple(laters), tuple(accs)))
        accs = st[3]
    for h in range(heads):
        g = g_ref[0, :, h * kb:(h + 1) * kb].astype(F32)
        o_ref[0, :, h * kb:(h + 1) * kb] = (accs[h] * _silu(g)).astype(o_ref.dtype)


def _sb_attention_sample(q, k_new, v_new, g, cache_k, cache_v, layer, wblk, name):
    b, ss, w = q.shape
    nl, _, past, heads, hd = cache_k.shape
    assert hd == SB_KBLOCK and w == heads * hd and ss <= SB_KBLOCK and ss % 16 == 0
    wblk = min(wblk, past // SB_KBLOCK)
    tail = wblk * SB_KBLOCK
    assert past % tail == 0
    nrem = past // SB_KBLOCK - wblk
    ck = cache_k.reshape(nl, b, past * heads, hd)
    cv = cache_v.reshape(nl, b, past * heads, hd)
    tok = lambda bi: (bi, 0, 0)
    tail_map = lambda bi: (layer, bi, past // tail - 1, 0)
    kern = functools.partial(_sb_sample_kernel, heads=heads, wblk=wblk, nrem=nrem, layer=layer)
    return pl.pallas_call(
        kern,
        grid=(b,),
        in_specs=[
            pl.BlockSpec((1, ss, w), tok),
            pl.BlockSpec((1, ss, w), tok),
            pl.BlockSpec((1, ss, w), tok),
            pl.BlockSpec((1, ss, w), tok),
            pl.BlockSpec((None, 1, tail * heads, hd), tail_map),
            pl.BlockSpec((None, 1, tail * heads, hd), tail_map),
            pl.BlockSpec(memory_space=pl.ANY),
            pl.BlockSpec(memory_space=pl.ANY),
            pl.BlockSpec((2 * SB_KBLOCK, 2 * SB_KBLOCK), lambda bi: (0, 0)),
        ],
        out_specs=pl.BlockSpec((1, ss, w), tok),
        out_shape=jax.ShapeDtypeStruct((b, ss, w), BF16),
        scratch_shapes=[pltpu.VMEM((SB_KBLOCK * heads, hd), F32), pltpu.VMEM((SB_KBLOCK * heads, hd), F32),
                        pltpu.SemaphoreType.DMA((2,))],
        compiler_params=_cparams(("arbitrary",)),
        name=name,
    )(q, k_new, v_new, g, ck, cv, ck, cv, _sb_tri())


def _swa_kernel(q_ref, kp_ref, kc_ref, vp_ref, vc_ref, g_ref, bias_ref, sink_ref, o_ref,
                *, cq, nch, span, kvh, rep, hd, mask_first):
    i = pl.program_id(1)
    kbuf = jnp.concatenate([kp_ref[0], kc_ref[0]], axis=0)
    vbuf = jnp.concatenate([vp_ref[0], vc_ref[0]], axis=0)
    nprev = kp_ref.shape[1]
    gw = rep * hd
    for kv in range(kvh):
        kk = kbuf[:, kv * hd:(kv + 1) * hd]
        vv = vbuf[:, kv * hd:(kv + 1) * hd]
        bias = bias_ref[kv]
        sink = sink_ref[kv]
        for j in range(nch):
            ks = kk[j * cq:j * cq + span]
            vs = vv[j * cq:j * cq + span]
            qs = q_ref[0, j * cq:(j + 1) * cq, kv * gw:(kv + 1) * gw]
            lhs = jnp.concatenate([qs[:, r * hd:(r + 1) * hd] for r in range(rep)], axis=0)
            st = lax.dot_general(ks, lhs, (((1,), (1,)), ((), ())), preferred_element_type=F32) + bias
            if mask_first and j * cq < nprev:
                buf_row = j * cq + lax.broadcasted_iota(jnp.int32, st.shape, 0)
                valid = jnp.logical_or(i > 0, buf_row >= nprev)
                st = jnp.where(valid, st, -jnp.inf)
            m = jnp.maximum(jnp.max(st, axis=0, keepdims=True), sink)
            p = jnp.exp2(st - m)
            den = jnp.sum(p, axis=0, keepdims=True) + jnp.exp2(sink - m)
            p = (p * (1.0 / den)).astype(BF16)
            o = lax.dot_general(p, vs, (((0,), (0,)), ((), ())), preferred_element_type=F32)
            ot = jnp.concatenate([o[r * cq:(r + 1) * cq] for r in range(rep)], axis=1)
            g = g_ref[0, j * cq:(j + 1) * cq, kv * gw:(kv + 1) * gw].astype(F32)
            o_ref[0, j * cq:(j + 1) * cq, kv * gw:(kv + 1) * gw] = (ot * _silu(g)).astype(o_ref.dtype)


def _alibi_bias_t(kvh, rep, cq, span, nprev):
    n = kvh * rep
    slopes = 2.0 ** (-8.0 * jnp.arange(1, n + 1, dtype=F32) / n)
    qi = jnp.arange(cq, dtype=jnp.int32)
    kj = jnp.arange(span, dtype=jnp.int32)
    dist = jnp.abs(qi[None, :] + nprev - kj[:, None]).astype(F32)
    bias = -(slopes * LOG2E).reshape(kvh, 1, rep, 1) * dist[None, :, None, :]
    return bias.reshape(kvh, span, rep * cq)


def _swa_attention(q, k_prev_src, k_cur, v_prev_src, v_cur, g, sinks, *, cq, nch, nprev, prev_from_cur, name):
    b, sq, qw = q.shape
    kvw = k_cur.shape[2]
    kvh, rep = sinks.shape
    hd = kvw // kvh
    rows = cq * nch
    span = nprev + cq
    assert sq % rows == 0 and qw == kvh * rep * hd
    if prev_from_cur:
        assert rows % nprev == 0
        ratio = rows // nprev
        prev_map = lambda bi, i: (bi, jnp.maximum(i * ratio - 1, 0), 0)
    else:
        assert sq == rows
        prev_map = lambda bi, i: (bi, 0, 0)
    cur_map = lambda bi, i: (bi, i, 0)
    const3 = lambda bi, i: (0, 0, 0)
    sink_t = jnp.broadcast_to((sinks.astype(F32) * LOG2E)[:, None, :, None], (kvh, 1, rep, cq)).reshape(kvh, 1, rep * cq)
    kern = functools.partial(_swa_kernel, cq=cq, nch=nch, span=span, kvh=kvh, rep=rep, hd=hd,
                             mask_first=prev_from_cur)
    return pl.pallas_call(
        kern,
        grid=(b, sq // rows),
        in_specs=[
            pl.BlockSpec((1, rows, qw), cur_map),
            pl.BlockSpec((1, nprev, kvw), prev_map),
            pl.BlockSpec((1, rows, kvw), cur_map),
            pl.BlockSpec((1, nprev, kvw), prev_map),
            pl.BlockSpec((1, rows, kvw), cur_map),
            pl.BlockSpec((1, rows, qw), cur_map),
            pl.BlockSpec((kvh, span, rep * cq), const3),
            pl.BlockSpec((kvh, 1, rep * cq), const3),
        ],
        out_specs=pl.BlockSpec((1, rows, qw), cur_map),
        out_shape=jax.ShapeDtypeStruct((b, sq, qw), BF16),
        compiler_params=_cparams(("parallel", "arbitrary")),
        name=name,
    )(q, k_prev_src, k_cur, v_prev_src, v_cur, g, _alibi_bias_t(kvh, rep, cq, span, nprev), sink_t)


def _ssm_kernel(u_ref, w_ref, m_ref, v_ref, lam_ref, h0_ref, y_ref, hfin_ref, s_scr, h_scr, carry, *, nchains):
    ti = pl.program_id(2)
    nrow = s_scr.shape[0]
    half = s_scr.shape[1] // 2
    steps = nrow // nchains

    @pl.when(ti == 0)
    def _():
        carry[...] = h0_ref[0, 0]

    u = u_ref[0, 0]
    s_scr[...] = jnp.dot(u, w_ref[0], preferred_element_type=F32)

    lam = lam_ref[0]
    lr = jnp.broadcast_to(lam[:, :half], (nchains, half))
    li = jnp.broadcast_to(lam[:, half:], (nchains, half))

    def step(c, h):
        r0 = pl.multiple_of(c * nchains, nchains)
        h_scr[pl.ds(r0, nchains), :] = h
        s = s_scr[pl.ds(r0, nchains), :]
        hr, hi = h[:, :half], h[:, half:]
        nr = lr * hr - li * hi + s[:, :half]
        ni = lr * hi + li * hr + s[:, half:]
        return jnp.concatenate([nr, ni], axis=1)

    h = lax.fori_loop(0, steps, step, carry[...], unroll=(8 if steps % 8 == 0 else 1))
    carry[...] = h
    hfin_ref[0, 0] = h

    y_ref[0, 0] = (jnp.dot(u, m_ref[0], preferred_element_type=F32)
                   + jnp.dot(h_scr[...].astype(BF16), v_ref[0], preferred_element_type=F32))


def _ssm_dense_kernel(u_ref, w_ref, m_ref, v_ref, lam_ref, h0_ref, y_ref, hfin_ref,
                      sre, sim, hre, him, cre, cim, *, nb, kt):
    ti = pl.program_id(2)
    rows = u_ref.shape[2]
    half = kt * LANES

    def fold(row_of):
        return jnp.concatenate([row_of(b, k) for b in range(nb) for k in range(kt)], axis=0)

    @pl.when(ti == 0)
    def _():
        cre[...] = fold(lambda b, k: h0_ref[b, 0, :, k * LANES:(k + 1) * LANES])
        cim[...] = fold(lambda b, k: h0_ref[b, 0, :, half + k * LANES:half + (k + 1) * LANES])

    us = [u_ref[0, b] for b in range(nb)]
    for b in range(nb):
        s = jnp.dot(us[b], w_ref[0], preferred_element_type=F32)
        for k in range(kt):
            sre[pl.ds(b * kt + k, rows, stride=SUBLANES), :] = s[:, k * LANES:(k + 1) * LANES]
            sim[pl.ds(b * kt + k, rows, stride=SUBLANES), :] = s[:, half + k * LANES:half + (k + 1) * LANES]

    lam = lam_ref[0]
    lr = fold(lambda b, k: lam[:, k * LANES:(k + 1) * LANES])
    li = fold(lambda b, k: lam[:, half + k * LANES:half + (k + 1) * LANES])

    def step(c, carry):
        hr, hi = carry
        r0 = pl.multiple_of(c * SUBLANES, SUBLANES)
        hre[pl.ds(r0, SUBLANES), :] = hr
        him[pl.ds(r0, SUBLANES), :] = hi
        nr = lr * hr - li * hi + sre[pl.ds(r0, SUBLANES), :]
        ni = lr * hi + li * hr + sim[pl.ds(r0, SUBLANES), :]
        return nr, ni

    hr, hi = lax.fori_loop(0, rows, step, (cre[...], cim[...]), unroll=(8 if rows % 8 == 0 else 1))
    cre[...] = hr
    cim[...] = hi
    for b in range(nb):
        pieces = ([hr[b * kt + k:b * kt + k + 1] for k in range(kt)]
                  + [hi[b * kt + k:b * kt + k + 1] for k in range(kt)])
        hfin_ref[b, 0] = jnp.concatenate(pieces, axis=1)
        hcat = jnp.concatenate(
            [hre[pl.ds(b * kt + k, rows, stride=SUBLANES), :] for k in range(kt)]
            + [him[pl.ds(b * kt + k, rows, stride=SUBLANES), :] for k in range(kt)], axis=1)
        y_ref[0, b] = (jnp.dot(us[b], m_ref[0], preferred_element_type=F32)
                       + jnp.dot(hcat.astype(BF16), v_ref[0], preferred_element_type=F32))


def _ssm_dense(u_rows, h0, tables, *, tile_rows, name):
    w, m, v, lam = tables
    ns, b, r, lw = u_rows.shape
    sw = w.shape[2]
    kt = sw // (2 * LANES)
    nb = SUBLANES // kt
    tile_rows = min(tile_rows, r)
    assert r % tile_rows == 0 and b % nb == 0 and nb * kt == SUBLANES
    row_map = lambda o, bi, ti: (o, bi, ti, 0)
    tab = lambda o, bi, ti: (o, 0, 0)
    st_map = lambda o, bi, ti: (bi, o, 0, 0)
    fold_scr = pltpu.VMEM((tile_rows * SUBLANES, LANES), F32)
    return pl.pallas_call(
        functools.partial(_ssm_dense_kernel, nb=nb, kt=kt),
        grid=(ns, b // nb, r // tile_rows),
        in_specs=[
            pl.BlockSpec((1, nb, tile_rows, lw), row_map),
            pl.BlockSpec((1, lw, sw), tab),
            pl.BlockSpec((1, lw, lw), tab),
            pl.BlockSpec((1, sw, lw), tab),
            pl.BlockSpec((1, 1, sw), tab),
            pl.BlockSpec((nb, 1, 1, sw), st_map),
        ],
        out_specs=[
            pl.BlockSpec((1, nb, tile_rows, lw), row_map),
            pl.BlockSpec((nb, 1, 1, sw), st_map),
        ],
        out_shape=[jax.ShapeDtypeStruct((ns, b, r, lw), F32),
                   jax.ShapeDtypeStruct((b, ns, 1, sw), F32)],
        scratch_shapes=[fold_scr, fold_scr, fold_scr, fold_scr,
                        pltpu.VMEM((SUBLANES, LANES), F32), pltpu.VMEM((SUBLANES, LANES), F32)],
        compiler_params=_cparams(("parallel", "parallel", "arbitrary")),
        name=name,
    )(u_rows, w, m, v, lam, h0)


def _ssm_tables(a_re, a_im, log_dt, b_re, b_im, c_re, c_im, L):
    g_, p_ = a_re.shape
    sg = SSM_SLAB_GROUPS
    ns = g_ // sg
    hp = lax.Precision.HIGHEST
    a_re = jnp.minimum(a_re.astype(F32), -1e-4)
    a_im = a_im.astype(F32)
    dt = jnp.exp(log_dt.astype(F32))[:, None]
    mag = jnp.exp(dt * a_re)
    lb_re = mag * jnp.cos(dt * a_im)
    lb_im = mag * jnp.sin(dt * a_im)
    n_re = lb_re - 1.0
    den = a_re * a_re + a_im * a_im
    f_re = ((n_re * a_re + lb_im * a_im) / den)[..., None]
    f_im = ((lb_im * a_re - n_re * a_im) / den)[..., None]
    br, bi = b_re.astype(F32), b_im.astype(F32)
    bb_re = f_re * br - f_im * bi
    bb_im = f_re * bi + f_im * br
    cr, ci = c_re.astype(F32), c_im.astype(F32)
    tau = jnp.arange(L + 1, dtype=F32)[:, None, None]
    pmag = jnp.exp(tau * dt * a_re)
    pw_re = pmag * jnp.cos(tau * dt * a_im)
    pw_im = pmag * jnp.sin(tau * dt * a_im)

    lb_r = pw_re[:, :, :, None] * bb_re[None] - pw_im[:, :, :, None] * bb_im[None]
    lb_i = pw_re[:, :, :, None] * bb_im[None] + pw_im[:, :, :, None] * bb_re[None]
    kt = (jnp.einsum('gop,tgpi->tgoi', cr, lb_r[:L], precision=hp)
          - jnp.einsum('gop,tgpi->tgoi', ci, lb_i[:L], precision=hp))
    lag = jnp.arange(L)[None, :] - jnp.arange(L)[:, None]
    kst = jnp.where((lag >= 0)[:, :, None, None, None], kt[jnp.clip(lag, 0, L - 1)], 0.0)

    rows_u = L * LANES
    cols_s = 2 * sg * p_
    rid = jnp.arange(rows_u)
    grp_u = (rid % LANES) // SSM_GROUP
    sid = jnp.arange(cols_s)
    grp_s = (sid % (sg * p_)) // p_

    def expand(compact, rep_mat, grp_r, grp_c):
        full = jnp.einsum('nrc,ck->nrk', compact, rep_mat, precision=hp)
        return jnp.where(grp_r[:, None] == grp_c[None, :], full, 0.0).astype(BF16)

    m_c = kst.reshape(L, L, ns, sg, SSM_GROUP, SSM_GROUP).transpose(2, 0, 3, 5, 1, 4).reshape(ns, rows_u, L * SSM_GROUP)
    cid = jnp.arange(rows_u)
    rep_to = ((cid[None, :] // LANES == jnp.arange(L * SSM_GROUP)[:, None] // SSM_GROUP)
              & (cid[None, :] % SSM_GROUP == jnp.arange(L * SSM_GROUP)[:, None] % SSM_GROUP)).astype(F32)
    m = expand(m_c, rep_to, grp_u, grp_u)
    wri = jnp.stack([lb_r[:L][::-1], lb_i[:L][::-1]], axis=0).reshape(2, L, ns, sg, p_, SSM_GROUP)
    w_c = wri.transpose(2, 1, 3, 5, 0, 4).reshape(ns, rows_u, 2 * p_)
    rep_ap = ((sid[None, :] // (sg * p_) == jnp.arange(2 * p_)[:, None] // p_)
              & (sid[None, :] % p_ == jnp.arange(2 * p_)[:, None] % p_)).astype(F32)
    w = expand(w_c, rep_ap, grp_u, grp_s)
    cl_r = cr[None] * pw_re[1:][:, :, None, :] - ci[None] * pw_im[1:][:, :, None, :]
    cl_i = cr[None] * pw_im[1:][:, :, None, :] + ci[None] * pw_re[1:][:, :, None, :]
    vri = jnp.stack([cl_r, -cl_i], axis=0).reshape(2, L, ns, sg, SSM_GROUP, p_)
    v_c = vri.transpose(2, 0, 3, 5, 1, 4).reshape(ns, cols_s, L * SSM_GROUP)
    v = expand(v_c, rep_to, grp_s, grp_u)
    lam = jnp.stack([pw_re[L], pw_im[L]], axis=0).reshape(2, ns, sg * p_)
    lam = jnp.transpose(lam, (1, 0, 2)).reshape(ns, 1, cols_s)
    return w, m, v, lam


def _ssm(u_rows, h0, tables, *, nchains, tile_rows, name):
    w, m, v, lam = tables
    ns, b, r, lw = u_rows.shape
    sw = w.shape[2]
    tile_rows = min(tile_rows, r)
    assert r % tile_rows == 0 and tile_rows % nchains == 0
    row_map = lambda o, bi, ti: (o, bi, ti, 0)
    tab = lambda o, bi, ti: (o, 0, 0)
    st_map = lambda o, bi, ti: (bi, o, 0, 0)
    return pl.pallas_call(
        functools.partial(_ssm_kernel, nchains=nchains),
        grid=(ns, b, r // tile_rows),
        in_specs=[
            pl.BlockSpec((1, 1, tile_rows, lw), row_map),
            pl.BlockSpec((1, lw, sw), tab),
            pl.BlockSpec((1, lw, lw), tab),
            pl.BlockSpec((1, sw, lw), tab),
            pl.BlockSpec((1, 1, sw), tab),
            pl.BlockSpec((1, 1, nchains, sw), st_map),
        ],
        out_specs=[
            pl.BlockSpec((1, 1, tile_rows, lw), row_map),
            pl.BlockSpec((1, 1, nchains, sw), st_map),
        ],
        out_shape=[jax.ShapeDtypeStruct((ns, b, r, lw), F32),
                   jax.ShapeDtypeStruct((b, ns, nchains, sw), F32)],
        scratch_shapes=[pltpu.VMEM((tile_rows, sw), F32), pltpu.VMEM((tile_rows, sw), F32),
                        pltpu.VMEM((nchains, sw), F32)],
        compiler_params=_cparams(("parallel", "parallel", "arbitrary")),
        name=name,
    )(u_rows, w, m, v, lam, h0)


def _gelu_tanh(x):
    return 0.5 * x * (1.0 + jnp.tanh(math.sqrt(2.0 / math.pi) * (x + 0.044715 * (x * x * x))))


def _s5_tail_kernel(y_ref, u_ref, g_ref, d_ref, wg_ref, wo_ref, x_ref, lg_ref, lb_ref, o_ref, y_scr, u_scr, *, alpha):
    ns, rows, _ = y_ref.shape
    for o in range(ns):
        for s in range(SUBLANES):
            y_scr[o, pl.ds(s, rows, stride=SUBLANES), :] = y_ref[o, :, s * LANES:(s + 1) * LANES]
            u_scr[o, pl.ds(s, rows, stride=SUBLANES), :] = u_ref[o, :, s * LANES:(s + 1) * LANES].astype(F32)
    y_tok = jnp.concatenate([y_scr[o] for o in range(ns)], axis=1)
    u_tok = jnp.concatenate([u_scr[o] for o in range(ns)], axis=1)
    y = _gelu_tanh(y_tok + d_ref[...] * u_tok)
    z = jnp.dot(y.astype(BF16), wg_ref[...], preferred_element_type=F32)
    y = y * (1.0 / (1.0 + jnp.exp(-z)))
    a = (y * _silu(g_ref[...].astype(F32))).astype(BF16)
    out = jnp.dot(a, wo_ref[...], preferred_element_type=F32)
    r = alpha * x_ref[...] + out
    o_ref[...] = _layer_norm_rows(r, lg_ref[...], lb_ref[...])


def _s5_tail(y_rows, u_rows, g2d, d_vec, w_glu, w_out, x2d, ln_g, ln_b, alpha, tm, name):
    ns, tr, lw = y_rows.shape
    t = tr * SUBLANES
    c = ns * LANES
    d = x2d.shape[1]
    tm = min(tm, t)
    assert t % tm == 0 and tm % SUBLANES == 0 and lw == SUBLANES * LANES
    row_spec = pl.BlockSpec((ns, tm // SUBLANES, lw), lambda i: (0, i, 0))
    const = lambda i: (0, 0)
    return pl.pallas_call(
        functools.partial(_s5_tail_kernel, alpha=alpha),
        grid=(t // tm,),
        in_specs=[
            row_spec, row_spec,
            pl.BlockSpec((tm, c), lambda i: (i, 0)),
            pl.BlockSpec((1, c), const),
            pl.BlockSpec((c, c), const, pipeline_mode=pl.Buffered(1)),
            pl.BlockSpec((c, d), const, pipeline_mode=pl.Buffered(1)),
            pl.BlockSpec((tm, d), lambda i: (i, 0)),
            pl.BlockSpec((1, d), const),
            pl.BlockSpec((1, d), const),
        ],
        out_specs=pl.BlockSpec((tm, d), lambda i: (i, 0)),
        out_shape=jax.ShapeDtypeStruct((t, d), F32),
        scratch_shapes=[pltpu.VMEM((ns, tm, LANES), F32), pltpu.VMEM((ns, tm, LANES), F32)],
        compiler_params=_cparams(("parallel",)),
        name=name,
    )(y_rows, u_rows, g2d, d_vec.reshape(1, c).astype(F32), w_glu, w_out, x2d,
      ln_g.reshape(1, d).astype(F32), ln_b.reshape(1, d).astype(F32))


TM_PROJ = 512
TM_OUT = 512
OUT_SUB_ROWS = 256
SB_CHAINS = 8
SB_SAMPLE_TAIL_BLOCKS = 2
SWA_CHUNKS_PER_STEP = 4
SSM_TILE_ROWS = 256


def _sb_layer(xp, xs, bp, bs, cache_k, cache_v, w_in, w_out, ln_g, ln_b, alpha, stack_p):
    layer = stack_p[0]
    heads, hd = cache_k.shape[3], cache_k.shape[4]
    width = heads * hd
    sp, ss = xp.shape[0] // bp, xs.shape[0] // bs
    w_in = w_in.astype(BF16)
    w_out = w_out.astype(BF16)

    def segs(kv_kind):
        return ((0, width, LOG2E * hd ** -0.5, (("plain", BF16),)),
                (width, width, 1.0, (("plain", BF16), (kv_kind, F32))),
                (2 * width, width, 1.0, (("plain", BF16), (kv_kind, F32))),
                (3 * width, width, 1.0, (("plain", BF16),)))

    r3 = lambda a, b_, s_: a.reshape(b_, s_, width)
    kv_kind = "stack_init" if stack_p[2] is None else "stack"
    q, kb, kf, vb, vf, g = _proj(xp, w_in, segs(kv_kind), TM_PROJ, "sb_proj_prompt", stack=stack_p)
    og = _sb_attention_prompt(r3(q, bp, sp), r3(kb, bp, sp), r3(vb, bp, sp), r3(g, bp, sp), heads,
                              nchain=min(SB_CHAINS, sp // SB_KBLOCK), name="sb_attn_prompt")
    xp_new = _outproj_ln(og.reshape(bp * sp, width), w_out, xp, ln_g, ln_b, alpha, TM_OUT, "sb_out_prompt")
    q, kb, kf_s, vb, vf_s, g = _proj(xs, w_in, segs("plain"), TM_PROJ, "sb_proj_sample")
    og = _sb_attention_sample(r3(q, bs, ss), r3(kb, bs, ss), r3(vb, bs, ss), r3(g, bs, ss), cache_k, cache_v,
                              layer, SB_SAMPLE_TAIL_BLOCKS, "sb_attn_sample")
    xs_new = _outproj_ln(og.reshape(bs * ss, width), w_out, xs, ln_g, ln_b, alpha, TM_OUT, "sb_out_sample")
    new_s = (kf_s.reshape(bs, ss, heads, hd), vf_s.reshape(bs, ss, heads, hd))
    return xp_new, xs_new, (kf, vf), new_s


def _swa_layer(xp, xs, bp, bs, cache_k, cache_v, w_in, sinks, w_out, ln_g, ln_b, alpha):
    kvh, hd = cache_k.shape[2], cache_k.shape[3]
    rep = sinks.shape[1]
    qw, kvw = kvh * rep * hd, kvh * hd
    nprev = cache_k.shape[1]
    sp, ss = xp.shape[0] // bp, xs.shape[0] // bs
    assert nprev == WINDOW and sp >= WINDOW
    w_in = w_in.astype(BF16)
    w_out = w_out.astype(BF16)
    segs = ((0, qw, LOG2E * hd ** -0.5, (("plain", BF16),)),
            (qw, kvw, 1.0, (("plain", BF16), ("plain", F32))),
            (qw + kvw, kvw, 1.0, (("plain", BF16), ("plain", F32))),
            (qw + 2 * kvw, qw, 1.0, (("plain", BF16),)))
    q, kb, kf, vb, vf, g = _proj(xp, w_in, segs, TM_PROJ, "swa_proj_prompt")
    k3, v3 = kb.reshape(bp, sp, kvw), vb.reshape(bp, sp, kvw)
    og = _swa_attention(q.reshape(bp, sp, qw), k3, k3, v3, v3, g.reshape(bp, sp, qw), sinks,
                        cq=CHUNK, nch=SWA_CHUNKS_PER_STEP, nprev=WINDOW, prev_from_cur=True,
                        name="swa_attn_prompt")
    xp_new = _outproj_ln(og.reshape(bp * sp, qw), w_out, xp, ln_g, ln_b, alpha, TM_OUT, "swa_out_prompt")
    last = lambda a: a.reshape(bp, sp, kvw)[:, sp - WINDOW:].reshape(bp, WINDOW, kvh, hd)
    new_p = (last(kf), last(vf))
    q, kb, kf, vb, vf, g = _proj(xs, w_in, segs, TM_PROJ, "swa_proj_sample")
    og = _swa_attention(q.reshape(bs, ss, qw),
                        cache_k.reshape(bs, nprev, kvw).astype(BF16), kb.reshape(bs, ss, kvw),
                        cache_v.reshape(bs, nprev, kvw).astype(BF16), vb.reshape(bs, ss, kvw),
                        g.reshape(bs, ss, qw), sinks,
                        cq=ss, nch=1, nprev=nprev, prev_from_cur=False, name="swa_attn_sample")
    xs_new = _outproj_ln(og.reshape(bs * ss, qw), w_out, xs, ln_g, ln_b, alpha, TM_OUT, "swa_out_sample")
    new_s = (kf.reshape(bs, ss, kvh, hd), vf.reshape(bs, ss, kvh, hd))
    return xp_new, xs_new, new_p, new_s


def _s5_layer(xp, xs, bp, bs, state_re, state_im, w_in, a_re, a_im, log_dt, b_re, b_im, c_re, c_im,
              d_vec, w_glu, w_out, ln_g, ln_b, alpha):
    groups, p_ = a_re.shape
    c = groups * SSM_GROUP
    ns = c // LANES
    sg = SSM_SLAB_GROUPS
    L = SSM_L
    lw = L * LANES
    sp, ss = xp.shape[0] // bp, xs.shape[0] // bs
    assert sp % L == 0 and ss % L == 0
    w_in = w_in.astype(BF16)
    w_glu = w_glu.astype(BF16)
    w_out = w_out.astype(BF16)
    tables = _ssm_tables(a_re, a_im, log_dt, b_re, b_im, c_re, c_im, L)
    segs = ((0, c, 1.0, (("rows", BF16),)), (c, c, 1.0, (("plain", BF16),)))

    def split_state(hfin):
        re = hfin[..., :sg * p_].reshape(hfin.shape[:-1] + (sg, p_))
        im = hfin[..., sg * p_:].reshape(hfin.shape[:-1] + (sg, p_))
        return re, im

    u, g = _proj(xp, w_in, segs, TM_PROJ, "s5_proj_prompt")
    h0 = jnp.zeros((bp, ns, 1, 2 * sg * p_), F32)
    seqs_per_step = SUBLANES // (sg * p_ // LANES)
    if bp % seqs_per_step == 0:
        y, hfin = _ssm_dense(u.reshape(ns, bp, sp // L, lw), h0, tables,
                             tile_rows=SSM_TILE_ROWS, name="s5_ssm_prompt")
    else:
        y, hfin = _ssm(u.reshape(ns, bp, sp // L, lw), h0, tables,
                       nchains=1, tile_rows=SSM_TILE_ROWS, name="s5_ssm_prompt")
    xp_new = _s5_tail(y.reshape(ns, bp * sp // L, lw), u, g, d_vec, w_glu, w_out, xp, ln_g, ln_b, alpha,
                      TM_OUT, "s5_tail_prompt")
    re, im = split_state(hfin[:, :, 0])
    new_p = (re.reshape(bp, groups, p_), im.reshape(bp, groups, p_))

    u, g = _proj(xs, w_in, segs, TM_PROJ, "s5_proj_sample")
    nck = ss // L
    u_rows = u.reshape(ns, bs, nck, lw).transpose(0, 2, 1, 3).reshape(ns, 1, nck * bs, lw)
    st = jnp.concatenate([state_re.astype(F32).reshape(bs, ns, sg * p_),
                          state_im.astype(F32).reshape(bs, ns, sg * p_)], axis=-1)
    h0 = st.transpose(1, 0, 2)[None]
    y, hfin = _ssm(u_rows, h0, tables, nchains=bs, tile_rows=nck * bs, name="s5_ssm_sample")
    y = y.reshape(ns, nck, bs, lw).transpose(0, 2, 1, 3).reshape(ns, bs * nck, lw)
    xs_new = _s5_tail(y, u, g, d_vec, w_glu, w_out, xs, ln_g, ln_b, alpha, TM_OUT, "s5_tail_sample")
    re, im = split_state(hfin[0].transpose(1, 0, 2))
    new_s = (re.reshape(bs, groups, p_), im.reshape(bs, groups, p_))
    return xp_new, xs_new, new_p, new_s


def kernel(x_prompt, x_sample, cache_k_a, cache_v_a, cache_k_b, cache_v_b, state_re_c, state_im_c, ln_g, ln_b, w_in_a, w_out_a, w_in_b, sinks_b, w_out_b, w_in_c, a_re_c, a_im_c, log_dt_c, b_re_c, b_im_c, c_re_c, c_im_c, d_c, w_glu_c, w_out_c):
    depth = ln_g.shape[0]
    alpha = (2 * depth) ** 0.25
    bp, sp, d = x_prompt.shape
    bs, ss, _ = x_sample.shape
    xp = x_prompt.reshape(bp * sp, d)
    xs = x_sample.reshape(bs * ss, d)
    n_a = cache_k_a.shape[0]
    heads_a, hd_a = cache_k_a.shape[3], cache_k_a.shape[4]
    outs = {k: [] for k in ("ka_s", "va_s", "kb_p", "vb_p", "kb_s", "vb_s", "hr_p", "hi_p", "hr_s", "hi_s")}
    sb_stack = None
    for i in range(depth):
        j, kind = i // N_MIXERS, i % N_MIXERS
        if kind == 0:
            xp, xs, sb_stack, new_s = _sb_layer(xp, xs, bp, bs, cache_k_a, cache_v_a, w_in_a[j], w_out_a[j],
                                                ln_g[i], ln_b[i], alpha, (j, n_a, sb_stack))
            new_p, names = (), ("ka_s", "va_s")
        elif kind == 1:
            xp, xs, new_p, new_s = _swa_layer(xp, xs, bp, bs, cache_k_b[j], cache_v_b[j], w_in_b[j], sinks_b[j],
                                              w_out_b[j], ln_g[i], ln_b[i], alpha)
            names = ("kb_p", "vb_p", "kb_s", "vb_s")
        else:
            xp, xs, new_p, new_s = _s5_layer(xp, xs, bp, bs, state_re_c[j], state_im_c[j], w_in_c[j], a_re_c[j],
                                             a_im_c[j], log_dt_c[j], b_re_c[j], b_im_c[j], c_re_c[j], c_im_c[j],
                                             d_c[j], w_glu_c[j], w_out_c[j], ln_g[i], ln_b[i], alpha)
            names = ("hr_p", "hi_p", "hr_s", "hi_s")
        for nm, val in zip(names, tuple(new_p) + tuple(new_s)):
            outs[nm].append(val)
    ka_p = sb_stack[0].reshape(n_a, bp, sp, heads_a, hd_a)
    va_p = sb_stack[1].reshape(n_a, bp, sp, heads_a, hd_a)
    return (xp.reshape(bp, sp, d), xs.reshape(bs, ss, d),
            ka_p, va_p, jnp.stack(outs["ka_s"]), jnp.stack(outs["va_s"]),
            jnp.stack(outs["kb_p"]), jnp.stack(outs["vb_p"]), jnp.stack(outs["kb_s"]), jnp.stack(outs["vb_s"]),
            jnp.stack(outs["hr_p"]), jnp.stack(outs["hi_p"]), jnp.stack(outs["hr_s"]), jnp.stack(outs["hi_s"]))
```

```python
import functools
import math

import jax
import jax.numpy as jnp
from jax import lax
from jax.experimental import pallas as pl
from jax.experimental.pallas import tpu as pltpu

F32 = jnp.float32
BF16 = jnp.bfloat16

CHUNK = 64
WINDOW = 128
N_MIXERS = 3
SSM_GROUP = 16
LN_EPS = 1e-5
SB_KBLOCK = 128

LANES = 128
SUBLANES = 8
VMEM_LIMIT = 56 * 1024 * 1024

LOG2E = math.log2(math.e)
SB_DEAD_LOG2 = -127.0
SSM_L = SUBLANES
SSM_SLAB_GROUPS = LANES // SSM_GROUP


def _silu(x):
    return x * (1.0 / (1.0 + jnp.exp(-x)))


def _cparams(sem, vmem=VMEM_LIMIT):
    return pltpu.CompilerParams(dimension_semantics=sem, vmem_limit_bytes=vmem)


def _proj_kernel(*refs, segs, n_alias, step, stack_layer):
    x_ref, w_ref = refs[0], refs[1]
    out_refs = refs[2 + n_alias:]
    xb = x_ref[...].astype(BF16)
    outs = iter(out_refs)
    scr = out_refs[-1]
    for c0, width, scale, kinds, *act in segs:
        o_refs = [next(outs) for _ in kinds]
        for s0 in range(0, width, step):
            s1 = min(width, s0 + step)
            acc = jnp.dot(xb, w_ref[:, c0 + s0:c0 + s1], preferred_element_type=F32)
            if scale != 1.0:
                acc = acc * scale
            if act:
                acc = _silu(acc)
            for (kind, _), o_ref in zip(kinds, o_refs):
                if kind == "rows":
                    rows = acc.shape[0] // SUBLANES
                    for l0 in range(s0, s1, LANES):
                        slab = (l0 - s0) // LANES
                        scr[slab] = acc[:, l0 - s0:l0 - s0 + LANES]
                        for s in range(SUBLANES):
                            piece = scr[slab, pl.ds(s, rows, stride=SUBLANES), :]
                            o_ref[l0 // LANES, :, s * LANES:(s + 1) * LANES] = piece.astype(o_ref.dtype)
                elif kind == "stack_init":
                    for layer in range(o_ref.shape[0]):
                        val = acc.astype(o_ref.dtype) if layer == stack_layer else jnp.zeros(acc.shape, o_ref.dtype)
                        o_ref[layer, :, s0:s1] = val
                else:
                    o_ref[:, s0:s1] = acc.astype(o_ref.dtype)


def _proj(x2d, w_bf16, segs, tm, name, stack=None):
    t, d = x2d.shape
    n = w_bf16.shape[1]
    tm = min(tm, t)
    assert t % tm == 0 and tm % SUBLANES == 0
    step = 1024
    out_specs, out_shape, alias_in = [], [], []
    need_scratch = False
    for _, wd, _, kinds, *_act in segs:
        for kind, dt in kinds:
            if kind == "rows":
                need_scratch = True
                out_specs.append(pl.BlockSpec((wd // LANES, tm // SUBLANES, SUBLANES * LANES), lambda i: (0, i, 0)))
                out_shape.append(jax.ShapeDtypeStruct((wd // LANES, t // SUBLANES, SUBLANES * LANES), dt))
            elif kind == "stack_init":
                out_specs.append(pl.BlockSpec((stack[1], tm, wd), lambda i: (0, i, 0)))
                out_shape.append(jax.ShapeDtypeStruct((stack[1], t, wd), dt))
            elif kind == "stack":
                j, nl, prev = stack
                out_specs.append(pl.BlockSpec((None, tm, wd), functools.partial(lambda i, j_: (j_, i, 0), j_=j)))
                out_shape.append(jax.ShapeDtypeStruct((nl, t, wd), dt))
                alias_in.append((len(out_shape) - 1, prev[len(alias_in)]))
            else:
                out_specs.append(pl.BlockSpec((tm, wd), lambda i: (i, 0)))
                out_shape.append(jax.ShapeDtypeStruct((t, wd), dt))
    in_specs = [
        pl.BlockSpec((tm, d), lambda i: (i, 0)),
        pl.BlockSpec((d, n), lambda i: (0, 0), pipeline_mode=pl.Buffered(1)),
    ] + [pl.BlockSpec(memory_space=pl.ANY) for _ in alias_in]
    scratch = [pltpu.VMEM((step // LANES, tm, LANES) if need_scratch else (1, SUBLANES, LANES), F32)]
    kern = functools.partial(_proj_kernel, segs=segs, n_alias=len(alias_in), step=step,
                             stack_layer=None if stack is None else stack[0])
    return pl.pallas_call(
        kern,
        grid=(t // tm,),
        in_specs=in_specs,
        out_specs=out_specs,
        out_shape=out_shape,
        scratch_shapes=scratch,
        input_output_aliases={2 + k: oi for k, (oi, _) in enumerate(alias_in)},
        compiler_params=_cparams(("parallel",)),
        name=name,
    )(x2d, w_bf16, *[buf for _, buf in alias_in])


def _layer_norm_rows(r, g, b):
    mu = jnp.mean(r, axis=-1, keepdims=True)
    rc = r - mu
    var = jnp.mean(rc * rc, axis=-1, keepdims=True)
    return rc * lax.rsqrt(var + LN_EPS) * g + b


def _outproj_kernel(a_ref, w_ref, x_ref, g_ref, b_ref, o_ref, *, alpha, sub):
    for r0 in range(0, a_ref.shape[0], sub):
        y = jnp.dot(a_ref[r0:r0 + sub, :], w_ref[...], preferred_element_type=F32)
        r = alpha * x_ref[r0:r0 + sub, :] + y
        o_ref[r0:r0 + sub, :] = _layer_norm_rows(r, g_ref[...], b_ref[...])


def _outproj_ln(a2d, w_bf16, x2d, ln_g, ln_b, alpha, tm, name):
    t, wdt = a2d.shape
    d = x2d.shape[1]
    tm = min(tm, t)
    assert t % tm == 0
    return pl.pallas_call(
        functools.partial(_outproj_kernel, alpha=alpha, sub=min(tm, OUT_SUB_ROWS)),
        grid=(t // tm,),
        in_specs=[
            pl.BlockSpec((tm, wdt), lambda i: (i, 0)),
            pl.BlockSpec((wdt, d), lambda i: (0, 0), pipeline_mode=pl.Buffered(1)),
            pl.BlockSpec((tm, d), lambda i: (i, 0)),
            pl.BlockSpec((1, d), lambda i: (0, 0)),
            pl.BlockSpec((1, d), lambda i: (0, 0)),
        ],
        out_specs=pl.BlockSpec((tm, d), lambda i: (i, 0)),
        out_shape=jax.ShapeDtypeStruct((t, d), F32),
        compiler_params=_cparams(("parallel",)),
        name=name,
    )(a2d, w_bf16, x2d, ln_g.reshape(1, d).astype(F32), ln_b.reshape(1, d).astype(F32))


def _scores(qs, kblks):
    return [lax.dot_general(q, k, (((1,), (1,)), ((), ())), preferred_element_type=F32) for q, k in zip(qs, kblks)]


def _sb_blocks(zs, vblks, masks, laters, accs, tri):
    n = len(zs)
    rows = zs[0].shape[0]
    lbetas, cats = [], []
    for c in range(n):
        z = zs[c]
        sp = jnp.maximum(z, 0.0) + jnp.log2(1.0 + jnp.exp2(-jnp.abs(z)))
        lbetas.append(z - sp)
        if masks[c] is not None:
            sp = jnp.where(masks[c], sp, 0.0)
        hi = sp.astype(BF16)
        lo = (sp - hi.astype(F32)).astype(BF16)
        cats.append(jnp.concatenate([hi, lo], axis=1))
    cs = jnp.dot(jnp.concatenate(cats, axis=0), tri, preferred_element_type=F32)
    new_l, ws = [], []
    for c in range(n):
        blk = cs[c * rows:(c + 1) * rows]
        w = jnp.exp2(lbetas[c] + blk[:, :SB_KBLOCK] + laters[c])
        if masks[c] is not None:
            w = jnp.where(masks[c], w, 0.0)
        ws.append(w.astype(BF16))
        new_l.append(laters[c] + blk[:, SB_KBLOCK:])
    new_a = [accs[c] + jnp.dot(ws[c], vblks[c], preferred_element_type=F32) for c in range(n)]
    return new_l, new_a


def _sb_alive(ls):
    m = ls[0]
    for l in ls[1:]:
        m = jnp.maximum(m, l)
    return jnp.max(m) > SB_DEAD_LOG2


def _sb_tri():
    j = jnp.arange(SB_KBLOCK)
    later = -(j[:, None] > j[None, :]).astype(BF16)
    ones = -jnp.ones((SB_KBLOCK, SB_KBLOCK), BF16)
    one = jnp.concatenate([later, ones], axis=1)
    return jnp.concatenate([one, one], axis=0)


def _sb_prompt_kernel(q_ref, k_ref, v_ref, g_ref, tri_ref, o_ref, *, nchain):
    kb = SB_KBLOCK
    i = pl.program_id(2)
    n0 = [i * nchain + c for c in range(nchain)]
    n0_max = n0[-1]
    col = lax.broadcasted_iota(jnp.int32, (kb, kb), 1)
    row = lax.broadcasted_iota(jnp.int32, (kb, kb), 0)

    def offs(nbs):
        return [pl.multiple_of(jnp.maximum(nb, 0) * kb, kb) for nb in nbs]

    def loadk(nbs):
        return [k_ref[0, pl.ds(o, kb), :] for o in offs(nbs)]

    def loadv(nbs):
        return [v_ref[0, pl.ds(o, kb), :] for o in offs(nbs)]

    def queries():
        return [q_ref[0, c * kb:(c + 1) * kb, :] for c in range(nchain)]

    zero = jnp.zeros((kb, kb), F32)
    diag = col < row
    laters, accs = _sb_blocks(_scores(queries(), loadk(n0)), loadv(n0), [diag] * nchain,
                              [zero] * nchain, [zero] * nchain, tri_ref[...])

    def cond(st):
        return jnp.logical_and(st[0] <= n0_max, st[1])

    def earlier_block(t, laters, accs):
        nbs = [n0[c] - t for c in range(nchain)]
        ls = [laters[c] + jnp.where(nbs[c] >= 0, 0.0, -1e30).astype(F32) for c in range(nchain)]
        return _sb_blocks(_scores(queries(), loadk(nbs)), loadv(nbs), [None] * nchain, ls, list(accs), tri_ref[...])

    laters, accs = earlier_block(1, laters, accs)

    def body(st):
        ls, as_ = earlier_block(st[0], st[2], st[3])
        return (st[0] + 1, _sb_alive(ls), tuple(ls), tuple(as_))

    st = lax.while_loop(cond, body, (jnp.int32(2), _sb_alive(laters), tuple(laters), tuple(accs)))
    accs = st[3]
    for c in range(nchain):
        g = g_ref[0, c * kb:(c + 1) * kb, :].astype(F32)
        o_ref[0, c * kb:(c + 1) * kb, :] = (accs[c] * g).astype(o_ref.dtype)


def _sb_attention_prompt(q, k, v, g, heads, nchain, name):
    b, s, w = q.shape
    hd = w // heads
    tq = SB_KBLOCK * nchain
    assert hd == SB_KBLOCK and s % tq == 0
    qmap = lambda bi, h, i: (bi, i, h)
    kmap = lambda bi, h, i: (bi, 0, h)
    return pl.pallas_call(
        functools.partial(_sb_prompt_kernel, nchain=nchain),
        grid=(b, heads, s // tq),
        in_specs=[
            pl.BlockSpec((1, tq, hd), qmap),
            pl.BlockSpec((1, s, hd), kmap),
            pl.BlockSpec((1, s, hd), kmap),
            pl.BlockSpec((1, tq, hd), qmap),
            pl.BlockSpec((2 * SB_KBLOCK, 2 * SB_KBLOCK), lambda bi, h, i: (0, 0)),
        ],
        out_specs=pl.BlockSpec((1, tq, hd), qmap),
        out_shape=jax.ShapeDtypeStruct((b, s, w), BF16),
        compiler_params=_cparams(("parallel", "parallel", "arbitrary")),
        name=name,
    )(q, k, v, g, _sb_tri())


def _sb_sample_kernel(q_ref, kn_ref, vn_ref, g_ref, ck_ref, cv_ref, ckh_ref, cvh_ref, tri_ref, o_ref,
                      kscr, vscr, sem, *, heads, wblk, nrem, layer):
    kb = SB_KBLOCK
    b = pl.program_id(0)
    ss = q_ref.shape[1]
    qs = [q_ref[0, :, h * kb:(h + 1) * kb] for h in range(heads)]
    zpad = jnp.zeros((kb - ss, kb), BF16)
    col = lax.broadcasted_iota(jnp.int32, (ss, kb), 1)
    row = lax.broadcasted_iota(jnp.int32, (ss, kb), 0)
    new_mask = col < row
    kbl = [jnp.concatenate([kn_ref[0, :, h * kb:(h + 1) * kb], zpad], axis=0) for h in range(heads)]
    vbl = [jnp.concatenate([vn_ref[0, :, h * kb:(h + 1) * kb], zpad], axis=0) for h in range(heads)]
    zero = jnp.zeros((ss, kb), F32)
    laters, accs = _sb_blocks(_scores(qs, kbl), vbl, [new_mask] * heads, [zero] * heads, [zero] * heads, tri_ref[...])

    def head_blocks(ref, r0):
        return [ref[pl.ds(r0 + h, kb, stride=heads), :].astype(BF16) for h in range(heads)]

    for j in range(wblk):
        r0 = (wblk - 1 - j) * kb * heads
        laters, accs = _sb_blocks(_scores(qs, head_blocks(ck_ref.at[0], r0)), head_blocks(cv_ref.at[0], r0),
                                  [None] * heads, laters, accs, tri_ref[...])

    if nrem > 0:
        def cond(st):
            return jnp.logical_and(st[0] < nrem, st[1])

        def body(st):
            t = st[0]
            r0 = pl.multiple_of((nrem - 1 - t) * (kb * heads), kb * heads)
            ck = pltpu.make_async_copy(ckh_ref.at[layer, b, pl.ds(r0, kb * heads)], kscr, sem.at[0])
            cv = pltpu.make_async_copy(cvh_ref.at[layer, b, pl.ds(r0, kb * heads)], vscr, sem.at[1])
            ck.start()
            cv.start()
            ck.wait()
            cv.wait()
            ls, as_ = _sb_blocks(_scores(qs, head_blocks(kscr, 0)), head_blocks(vscr, 0), [None] * heads,
                                 list(st[2]), list(st[3]), tri_ref[...])
            return (t + 1, _sb_alive(ls), tuple(ls), tuple(as_))

        st = lax.while_loop(cond, body, (jnp.int32(0), _sb_alive(laters), tuple(laters), tuple(accs)))
        accs = st[3]
    for h in range(heads):
        g = g_ref[0, :, h * kb:(h + 1) * kb].astype(F32)
        o_ref[0, :, h * kb:(h + 1) * kb] = (accs[h] * g).astype(o_ref.dtype)


def _sb_attention_sample(q, k_new, v_new, g, cache_k, cache_v, layer, wblk, name):
    b, ss, w = q.shape
    nl, _, past, heads, hd = cache_k.shape
    assert hd == SB_KBLOCK and w == heads * hd and ss <= SB_KBLOCK and ss % 16 == 0
    wblk = min(wblk, past // SB_KBLOCK)
    tail = wblk * SB_KBLOCK
    assert past % tail == 0
    nrem = past // SB_KBLOCK - wblk
    ck = cache_k.reshape(nl, b, past * heads, hd)
    cv = cache_v.reshape(nl, b, past * heads, hd)
    tok = lambda bi: (bi, 0, 0)
    tail_map = lambda bi: (layer, bi, past // tail - 1, 0)
    kern = functools.partial(_sb_sample_kernel, heads=heads, wblk=wblk, nrem=nrem, layer=layer)
    return pl.pallas_call(
        kern,
        grid=(b,),
        in_specs=[
            pl.BlockSpec((1, ss, w), tok),
            pl.BlockSpec((1, ss, w), tok),
            pl.BlockSpec((1, ss, w), tok),
            pl.BlockSpec((1, ss, w), tok),
            pl.BlockSpec((None, 1, tail * heads, hd), tail_map),
            pl.BlockSpec((None, 1, tail * heads, hd), tail_map),
            pl.BlockSpec(memory_space=pl.ANY),
            pl.BlockSpec(memory_space=pl.ANY),
            pl.BlockSpec((2 * SB_KBLOCK, 2 * SB_KBLOCK), lambda bi: (0, 0)),
        ],
        out_specs=pl.BlockSpec((1, ss, w), tok),
        out_shape=jax.ShapeDtypeStruct((b, ss, w), BF16),
        scratch_shapes=[pltpu.VMEM((SB_KBLOCK * heads, hd), F32), pltpu.VMEM((SB_KBLOCK * heads, hd), F32),
                        pltpu.SemaphoreType.DMA((2,))],
        compiler_params=_cparams(("arbitrary",)),
        name=name,
    )(q, k_new, v_new, g, ck, cv, ck, cv, _sb_tri())


def _swa_kernel(q_ref, kp_ref, kc_ref, vp_ref, vc_ref, g_ref, bias_ref, sink_ref, o_ref,
                *, cq, nch, span, kvh, rep, hd, mask_first):
    i = pl.program_id(1)
    kbuf = jnp.concatenate([kp_ref[0], kc_ref[0]], axis=0)
    vbuf = jnp.concatenate([vp_ref[0], vc_ref[0]], axis=0)
    nprev = kp_ref.shape[1]
    gw = rep * hd
    for kv in range(kvh):
        kk = kbuf[:, kv * hd:(kv + 1) * hd]
        vv = vbuf[:, kv * hd:(kv + 1) * hd]
        bias = bias_ref[kv]
        sink = sink_ref[kv]
        sts = []
        for j in range(nch):
            ks = kk[j * cq:j * cq + span]
            qs = q_ref[0, j * cq:(j + 1) * cq, kv * gw:(kv + 1) * gw]
            lhs = jnp.concatenate([qs[:, r * hd:(r + 1) * hd] for r in range(rep)], axis=0)
            st = lax.dot_general(ks, lhs, (((1,), (1,)), ((), ())), preferred_element_type=F32) + bias
            if mask_first and j * cq < nprev:
                buf_row = j * cq + lax.broadcasted_iota(jnp.int32, st.shape, 0)
                valid = jnp.logical_or(i > 0, buf_row >= nprev)
                st = jnp.where(valid, st, -jnp.inf)
            sts.append(st)
        ps = []
        for st in sts:
            m = jnp.maximum(jnp.max(st, axis=0, keepdims=True), sink)
            p = jnp.exp2(st - m)
            den = jnp.sum(p, axis=0, keepdims=True) + jnp.exp2(sink - m)
            ps.append((p * (1.0 / den)).astype(BF16))
        for j in range(nch):
            vs = vv[j * cq:j * cq + span]
            o = lax.dot_general(ps[j], vs, (((0,), (0,)), ((), ())), preferred_element_type=F32)
            ot = jnp.concatenate([o[r * cq:(r + 1) * cq] for r in range(rep)], axis=1)
            g = g_ref[0, j * cq:(j + 1) * cq, kv * gw:(kv + 1) * gw].astype(F32)
            o_ref[0, j * cq:(j + 1) * cq, kv * gw:(kv + 1) * gw] = (ot * g).astype(o_ref.dtype)


def _alibi_bias_t(kvh, rep, cq, span, nprev):
    n = kvh * rep
    slopes = 2.0 ** (-8.0 * jnp.arange(1, n + 1, dtype=F32) / n)
    qi = jnp.arange(cq, dtype=jnp.int32)
    kj = jnp.arange(span, dtype=jnp.int32)
    dist = jnp.abs(qi[None, :] + nprev - kj[:, None]).astype(F32)
    bias = -(slopes * LOG2E).reshape(kvh, 1, rep, 1) * dist[None, :, None, :]
    return bias.reshape(kvh, span, rep * cq)


def _swa_attention(q, k_prev_src, k_cur, v_prev_src, v_cur, g, sinks, *, cq, nch, nprev, prev_from_cur, name):
    b, sq, qw = q.shape
    kvw = k_cur.shape[2]
    kvh, rep = sinks.shape
    hd = kvw // kvh
    rows = cq * nch
    span = nprev + cq
    assert sq % rows == 0 and qw == kvh * rep * hd
    if prev_from_cur:
        assert rows % nprev == 0
        ratio = rows // nprev
        prev_map = lambda bi, i: (bi, jnp.maximum(i * ratio - 1, 0), 0)
    else:
        assert sq == rows
        prev_map = lambda bi, i: (bi, 0, 0)
    cur_map = lambda bi, i: (bi, i, 0)
    const3 = lambda bi, i: (0, 0, 0)
    sink_t = jnp.broadcast_to((sinks.astype(F32) * LOG2E)[:, None, :, None], (kvh, 1, rep, cq)).reshape(kvh, 1, rep * cq)
    kern = functools.partial(_swa_kernel, cq=cq, nch=nch, span=span, kvh=kvh, rep=rep, hd=hd,
                             mask_first=prev_from_cur)
    return pl.pallas_call(
        kern,
        grid=(b, sq // rows),
        in_specs=[
            pl.BlockSpec((1, rows, qw), cur_map),
            pl.BlockSpec((1, nprev, kvw), prev_map),
            pl.BlockSpec((1, rows, kvw), cur_map),
            pl.BlockSpec((1, nprev, kvw), prev_map),
            pl.BlockSpec((1, rows, kvw), cur_map),
            pl.BlockSpec((1, rows, qw), cur_map),
            pl.BlockSpec((kvh, span, rep * cq), const3),
            pl.BlockSpec((kvh, 1, rep * cq), const3),
        ],
        out_specs=pl.BlockSpec((1, rows, qw), cur_map),
        out_shape=jax.ShapeDtypeStruct((b, sq, qw), BF16),
        compiler_params=_cparams(("parallel", "arbitrary")),
        name=name,
    )(q, k_prev_src, k_cur, v_prev_src, v_cur, g, _alibi_bias_t(kvh, rep, cq, span, nprev), sink_t)


def _ssm_kernel(u_ref, w_ref, m_ref, v_ref, lam_ref, h0_ref, y_ref, hfin_ref, s_scr, h_scr, carry, *, nchains):
    ti = pl.program_id(2)
    nrow = s_scr.shape[0]
    half = s_scr.shape[1] // 2
    steps = nrow // nchains

    @pl.when(ti == 0)
    def _():
        carry[...] = h0_ref[0, 0]

    u = u_ref[0, 0]
    s_scr[...] = jnp.dot(u, w_ref[0], preferred_element_type=F32)

    lam = lam_ref[0]
    lr = jnp.broadcast_to(lam[:, :half], (nchains, half))
    li = jnp.broadcast_to(lam[:, half:], (nchains, half))

    def step(c, h):
        r0 = pl.multiple_of(c * nchains, nchains)
        h_scr[pl.ds(r0, nchains), :] = h
        s = s_scr[pl.ds(r0, nchains), :]
        hr, hi = h[:, :half], h[:, half:]
        nr = lr * hr - li * hi + s[:, :half]
        ni = lr * hi + li * hr + s[:, half:]
        return jnp.concatenate([nr, ni], axis=1)

    h = lax.fori_loop(0, steps, step, carry[...], unroll=(8 if steps % 8 == 0 else 1))
    carry[...] = h
    hfin_ref[0, 0] = h

    y_ref[0, 0] = (jnp.dot(u, m_ref[0], preferred_element_type=F32)
                   + jnp.dot(h_scr[...].astype(BF16), v_ref[0], preferred_element_type=F32))


def _ssm_dense_kernel(u_ref, w_ref, m_ref, v_ref, lam_ref, h0_ref, y_ref, hfin_ref,
                      sre, sim, hre, him, cre, cim, *, nb, kt):
    ti = pl.program_id(2)
    rows = u_ref.shape[2]
    half = kt * LANES

    def fold(row_of):
        return jnp.concatenate([row_of(b, k) for b in range(nb) for k in range(kt)], axis=0)

    @pl.when(ti == 0)
    def _():
        cre[...] = fold(lambda b, k: h0_ref[b, 0, :, k * LANES:(k + 1) * LANES])
        cim[...] = fold(lambda b, k: h0_ref[b, 0, :, half + k * LANES:half + (k + 1) * LANES])

    us = [u_ref[0, b] for b in range(nb)]
    for b in range(nb):
        s = jnp.dot(us[b], w_ref[0], preferred_element_type=F32)
        for k in range(kt):
            sre[pl.ds(b * kt + k, rows, stride=SUBLANES), :] = s[:, k * LANES:(k + 1) * LANES]
            sim[pl.ds(b * kt + k, rows, stride=SUBLANES), :] = s[:, half + k * LANES:half + (k + 1) * LANES]

    lam = lam_ref[0]
    lr = fold(lambda b, k: lam[:, k * LANES:(k + 1) * LANES])
    li = fold(lambda b, k: lam[:, half + k * LANES:half + (k + 1) * LANES])

    def step(c, carry):
        hr, hi = carry
        r0 = pl.multiple_of(c * SUBLANES, SUBLANES)
        hre[pl.ds(r0, SUBLANES), :] = hr
        him[pl.ds(r0, SUBLANES), :] = hi
        nr = lr * hr - li * hi + sre[pl.ds(r0, SUBLANES), :]
        ni = lr * hi + li * hr + sim[pl.ds(r0, SUBLANES), :]
        return nr, ni

    hr, hi = lax.fori_loop(0, rows, step, (cre[...], cim[...]), unroll=(8 if rows % 8 == 0 else 1))
    cre[...] = hr
    cim[...] = hi
    for b in range(nb):
        pieces = ([hr[b * kt + k:b * kt + k + 1] for k in range(kt)]
                  + [hi[b * kt + k:b * kt + k + 1] for k in range(kt)])
        hfin_ref[b, 0] = jnp.concatenate(pieces, axis=1)
        hcat = jnp.concatenate(
            [hre[pl.ds(b * kt + k, rows, stride=SUBLANES), :] for k in range(kt)]
            + [him[pl.ds(b * kt + k, rows, stride=SUBLANES), :] for k in range(kt)], axis=1)
        y_ref[0, b] = (jnp.dot(us[b], m_ref[0], preferred_element_type=F32)
                       + jnp.dot(hcat.astype(BF16), v_ref[0], preferred_element_type=F32))


def _ssm_dense(u_rows, h0, tables, *, tile_rows, name):
    w, m, v, lam = tables
    ns, b, r, lw = u_rows.shape
    sw = w.shape[2]
    kt = sw // (2 * LANES)
    nb = SUBLANES // kt
    tile_rows = min(tile_rows, r)
    assert r % tile_rows == 0 and b % nb == 0 and nb * kt == SUBLANES
    row_map = lambda o, bi, ti: (o, bi, ti, 0)
    tab = lambda o, bi, ti: (o, 0, 0)
    st_map = lambda o, bi, ti: (bi, o, 0, 0)
    fold_scr = pltpu.VMEM((tile_rows * SUBLANES, LANES), F32)
    return pl.pallas_call(
        functools.partial(_ssm_dense_kernel, nb=nb, kt=kt),
        grid=(ns, b // nb, r // tile_rows),
        in_specs=[
            pl.BlockSpec((1, nb, tile_rows, lw), row_map),
            pl.BlockSpec((1, lw, sw), tab),
            pl.BlockSpec((1, lw, lw), tab),
            pl.BlockSpec((1, sw, lw), tab),
            pl.BlockSpec((1, 1, sw), tab),
            pl.BlockSpec((nb, 1, 1, sw), st_map),
        ],
        out_specs=[
            pl.BlockSpec((1, nb, tile_rows, lw), row_map),
            pl.BlockSpec((nb, 1, 1, sw), st_map),
        ],
        out_shape=[jax.ShapeDtypeStruct((ns, b, r, lw), F32),
                   jax.ShapeDtypeStruct((b, ns, 1, sw), F32)],
        scratch_shapes=[fold_scr, fold_scr, fold_scr, fold_scr,
                        pltpu.VMEM((SUBLANES, LANES), F32), pltpu.VMEM((SUBLANES, LANES), F32)],
        compiler_params=_cparams(("parallel", "parallel", "arbitrary")),
        name=name,
    )(u_rows, w, m, v, lam, h0)


def _ssm_tables(a_re, a_im, log_dt, b_re, b_im, c_re, c_im, L):
    g_, p_ = a_re.shape
    sg = SSM_SLAB_GROUPS
    ns = g_ // sg
    hp = lax.Precision.HIGHEST
    a_re = jnp.minimum(a_re.astype(F32), -1e-4)
    a_im = a_im.astype(F32)
    dt = jnp.exp(log_dt.astype(F32))[:, None]
    mag = jnp.exp(dt * a_re)
    lb_re = mag * jnp.cos(dt * a_im)
    lb_im = mag * jnp.sin(dt * a_im)
    n_re = lb_re - 1.0
    den = a_re * a_re + a_im * a_im
    f_re = ((n_re * a_re + lb_im * a_im) / den)[..., None]
    f_im = ((lb_im * a_re - n_re * a_im) / den)[..., None]
    br, bi = b_re.astype(F32), b_im.astype(F32)
    bb_re = f_re * br - f_im * bi
    bb_im = f_re * bi + f_im * br
    cr, ci = c_re.astype(F32), c_im.astype(F32)
    tau = jnp.arange(L + 1, dtype=F32)[:, None, None]
    pmag = jnp.exp(tau * dt * a_re)
    pw_re = pmag * jnp.cos(tau * dt * a_im)
    pw_im = pmag * jnp.sin(tau * dt * a_im)

    lb_r = pw_re[:, :, :, None] * bb_re[None] - pw_im[:, :, :, None] * bb_im[None]
    lb_i = pw_re[:, :, :, None] * bb_im[None] + pw_im[:, :, :, None] * bb_re[None]
    kt = (jnp.einsum('gop,tgpi->tgoi', cr, lb_r[:L], precision=hp)
          - jnp.einsum('gop,tgpi->tgoi', ci, lb_i[:L], precision=hp))
    lag = jnp.arange(L)[None, :] - jnp.arange(L)[:, None]
    kst = jnp.where((lag >= 0)[:, :, None, None, None], kt[jnp.clip(lag, 0, L - 1)], 0.0)

    rows_u = L * LANES
    cols_s = 2 * sg * p_
    rid = jnp.arange(rows_u)
    grp_u = (rid % LANES) // SSM_GROUP
    sid = jnp.arange(cols_s)
    grp_s = (sid % (sg * p_)) // p_

    def expand(compact, rep_mat, grp_r, grp_c):
        full = jnp.einsum('nrc,ck->nrk', compact, rep_mat, precision=hp)
        return jnp.where(grp_r[:, None] == grp_c[None, :], full, 0.0).astype(BF16)

    m_c = kst.reshape(L, L, ns, sg, SSM_GROUP, SSM_GROUP).transpose(2, 0, 3, 5, 1, 4).reshape(ns, rows_u, L * SSM_GROUP)
    cid = jnp.arange(rows_u)
    rep_to = ((cid[None, :] // LANES == jnp.arange(L * SSM_GROUP)[:, None] // SSM_GROUP)
              & (cid[None, :] % SSM_GROUP == jnp.arange(L * SSM_GROUP)[:, None] % SSM_GROUP)).astype(F32)
    m = expand(m_c, rep_to, grp_u, grp_u)
    wri = jnp.stack([lb_r[:L][::-1], lb_i[:L][::-1]], axis=0).reshape(2, L, ns, sg, p_, SSM_GROUP)
    w_c = wri.transpose(2, 1, 3, 5, 0, 4).reshape(ns, rows_u, 2 * p_)
    rep_ap = ((sid[None, :] // (sg * p_) == jnp.arange(2 * p_)[:, None] // p_)
              & (sid[None, :] % p_ == jnp.arange(2 * p_)[:, None] % p_)).astype(F32)
    w = expand(w_c, rep_ap, grp_u, grp_s)
    cl_r = cr[None] * pw_re[1:][:, :, None, :] - ci[None] * pw_im[1:][:, :, None, :]
    cl_i = cr[None] * pw_im[1:][:, :, None, :] + ci[None] * pw_re[1:][:, :, None, :]
    vri = jnp.stack([cl_r, -cl_i], axis=0).reshape(2, L, ns, sg, SSM_GROUP, p_)
    v_c = vri.transpose(2, 0, 3, 5, 1, 4).reshape(ns, cols_s, L * SSM_GROUP)
    v = expand(v_c, rep_to, grp_s, grp_u)
    lam = jnp.stack([pw_re[L], pw_im[L]], axis=0).reshape(2, ns, sg * p_)
    lam = jnp.transpose(lam, (1, 0, 2)).reshape(ns, 1, cols_s)
    return w, m, v, lam


def _ssm(u_rows, h0, tables, *, nchains, tile_rows, name):
    w, m, v, lam = tables
    ns, b, r, lw = u_rows.shape
    sw = w.shape[2]
    tile_rows = min(tile_rows, r)
    assert r % tile_rows == 0 and tile_rows % nchains == 0
    row_map = lambda o, bi, ti: (o, bi, ti, 0)
    tab = lambda o, bi, ti: (o, 0, 0)
    st_map = lambda o, bi, ti: (bi, o, 0, 0)
    return pl.pallas_call(
        functools.partial(_ssm_kernel, nchains=nchains),
        grid=(ns, b, r // tile_rows),
        in_specs=[
            pl.BlockSpec((1, 1, tile_rows, lw), row_map),
            pl.BlockSpec((1, lw, sw), tab),
            pl.BlockSpec((1, lw, lw), tab),
            pl.BlockSpec((1, sw, lw), tab),
            pl.BlockSpec((1, 1, sw), tab),
            pl.BlockSpec((1, 1, nchains, sw), st_map),
        ],
        out_specs=[
            pl.BlockSpec((1, 1, tile_rows, lw), row_map),
            pl.BlockSpec((1, 1, nchains, sw), st_map),
        ],
        out_shape=[jax.ShapeDtypeStruct((ns, b, r, lw), F32),
                   jax.ShapeDtypeStruct((b, ns, nchains, sw), F32)],
        scratch_shapes=[pltpu.VMEM((tile_rows, sw), F32), pltpu.VMEM((tile_rows, sw), F32),
                        pltpu.VMEM((nchains, sw), F32)],
        compiler_params=_cparams(("parallel", "parallel", "arbitrary")),
        name=name,
    )(u_rows, w, m, v, lam, h0)


def _gelu_tanh(x):
    return 0.5 * x * (1.0 + jnp.tanh(math.sqrt(2.0 / math.pi) * (x + 0.044715 * (x * x * x))))


def _s5_tail_kernel(y_ref, u_ref, g_ref, d_ref, wg_ref, wo_ref, x_ref, lg_ref, lb_ref, o_ref, y_scr, u_scr, *, alpha):
    ns, rows, _ = y_ref.shape
    for o in range(ns):
        for s in range(SUBLANES):
            y_scr[o, pl.ds(s, rows, stride=SUBLANES), :] = y_ref[o, :, s * LANES:(s + 1) * LANES]
            u_scr[o, pl.ds(s, rows, stride=SUBLANES), :] = u_ref[o, :, s * LANES:(s + 1) * LANES].astype(F32)
    y_tok = jnp.concatenate([y_scr[o] for o in range(ns)], axis=1)
    u_tok = jnp.concatenate([u_scr[o] for o in range(ns)], axis=1)
    y = _gelu_tanh(y_tok + d_ref[...] * u_tok)
    z = jnp.dot(y.astype(BF16), wg_ref[...], preferred_element_type=F32)
    y = y * (1.0 / (1.0 + jnp.exp(-z)))
    a = (y * g_ref[...].astype(F32)).astype(BF16)
    out = jnp.dot(a, wo_ref[...], preferred_element_type=F32)
    r = alpha * x_ref[...] + out
    o_ref[...] = _layer_norm_rows(r, lg_ref[...], lb_ref[...])


def _s5_tail(y_rows, u_rows, g2d, d_vec, w_glu, w_out, x2d, ln_g, ln_b, alpha, tm, name):
    ns, tr, lw = y_rows.shape
    t = tr * SUBLANES
    c = ns * LANES
    d = x2d.shape[1]
    tm = min(tm, t)
    assert t % tm == 0 and tm % SUBLANES == 0 and lw == SUBLANES * LANES
    row_spec = pl.BlockSpec((ns, tm // SUBLANES, lw), lambda i: (0, i, 0))
    const = lambda i: (0, 0)
    return pl.pallas_call(
        functools.partial(_s5_tail_kernel, alpha=alpha),
        grid=(t // tm,),
        in_specs=[
            row_spec, row_spec,
            pl.BlockSpec((tm, c), lambda i: (i, 0)),
            pl.BlockSpec((1, c), const),
            pl.BlockSpec((c, c), const, pipeline_mode=pl.Buffered(1)),
            pl.BlockSpec((c, d), const, pipeline_mode=pl.Buffered(1)),
            pl.BlockSpec((tm, d), lambda i: (i, 0)),
            pl.BlockSpec((1, d), const),
            pl.BlockSpec((1, d), const),
        ],
        out_specs=pl.BlockSpec((tm, d), lambda i: (i, 0)),
        out_shape=jax.ShapeDtypeStruct((t, d), F32),
        scratch_shapes=[pltpu.VMEM((ns, tm, LANES), F32), pltpu.VMEM((ns, tm, LANES), F32)],
        compiler_params=_cparams(("parallel",)),
        name=name,
    )(y_rows, u_rows, g2d, d_vec.reshape(1, c).astype(F32), w_glu, w_out, x2d,
      ln_g.reshape(1, d).astype(F32), ln_b.reshape(1, d).astype(F32))


TM_PROJ = 512
TM_OUT = 512
OUT_SUB_ROWS = 256
SB_CHAINS = 8
SB_SAMPLE_TAIL_BLOCKS = 2
SWA_CHUNKS_PER_STEP = 4
SSM_TILE_ROWS = 512


def _sb_layer(xp, xs, bp, bs, cache_k, cache_v, w_in, w_out, ln_g, ln_b, alpha, stack_p):
    layer = stack_p[0]
    heads, hd = cache_k.shape[3], cache_k.shape[4]
    width = heads * hd
    sp, ss = xp.shape[0] // bp, xs.shape[0] // bs
    w_in = w_in.astype(BF16)
    w_out = w_out.astype(BF16)

    def segs(kv_kind):
        return ((3 * width, width, 1.0, (("plain", BF16),), "silu"),
                (0, width, LOG2E * hd ** -0.5, (("plain", BF16),)),
                (width, width, 1.0, (("plain", BF16), (kv_kind, F32))),
                (2 * width, width, 1.0, (("plain", BF16), (kv_kind, F32))))

    r3 = lambda a, b_, s_: a.reshape(b_, s_, width)
    kv_kind = "stack_init" if stack_p[2] is None else "stack"
    g, q, kb, kf, vb, vf = _proj(xp, w_in, segs(kv_kind), TM_PROJ, "sb_proj_prompt", stack=stack_p)
    og = _sb_attention_prompt(r3(q, bp, sp), r3(kb, bp, sp), r3(vb, bp, sp), r3(g, bp, sp), heads,
                              nchain=min(SB_CHAINS, sp // SB_KBLOCK), name="sb_attn_prompt")
    xp_new = _outproj_ln(og.reshape(bp * sp, width), w_out, xp, ln_g, ln_b, alpha, TM_OUT, "sb_out_prompt")
    g, q, kb, kf_s, vb, vf_s = _proj(xs, w_in, segs("plain"), TM_PROJ, "sb_proj_sample")
    og = _sb_attention_sample(r3(q, bs, ss), r3(kb, bs, ss), r3(vb, bs, ss), r3(g, bs, ss), cache_k, cache_v,
                              layer, SB_SAMPLE_TAIL_BLOCKS, "sb_attn_sample")
    xs_new = _outproj_ln(og.reshape(bs * ss, width), w_out, xs, ln_g, ln_b, alpha, TM_OUT, "sb_out_sample")
    new_s = (kf_s.reshape(bs, ss, heads, hd), vf_s.reshape(bs, ss, heads, hd))
    return xp_new, xs_new, (kf, vf), new_s


def _swa_layer(xp, xs, bp, bs, cache_k, cache_v, w_in, sinks, w_out, ln_g, ln_b, alpha):
    kvh, hd = cache_k.shape[2], cache_k.shape[3]
    rep = sinks.shape[1]
    qw, kvw = kvh * rep * hd, kvh * hd
    nprev = cache_k.shape[1]
    sp, ss = xp.shape[0] // bp, xs.shape[0] // bs
    assert nprev == WINDOW and sp >= WINDOW
    w_in = w_in.astype(BF16)
    w_out = w_out.astype(BF16)
    segs = ((qw + 2 * kvw, qw, 1.0, (("plain", BF16),), "silu"),
            (0, qw, LOG2E * hd ** -0.5, (("plain", BF16),)),
            (qw, kvw, 1.0, (("plain", BF16), ("plain", F32))),
            (qw + kvw, kvw, 1.0, (("plain", BF16), ("plain", F32))))
    g, q, kb, kf, vb, vf = _proj(xp, w_in, segs, TM_PROJ, "swa_proj_prompt")
    k3, v3 = kb.reshape(bp, sp, kvw), vb.reshape(bp, sp, kvw)
    og = _swa_attention(q.reshape(bp, sp, qw), k3, k3, v3, v3, g.reshape(bp, sp, qw), sinks,
                        cq=CHUNK, nch=SWA_CHUNKS_PER_STEP, nprev=WINDOW, prev_from_cur=True,
                        name="swa_attn_prompt")
    xp_new = _outproj_ln(og.reshape(bp * sp, qw), w_out, xp, ln_g, ln_b, alpha, TM_OUT, "swa_out_prompt")
    last = lambda a: a.reshape(bp, sp, kvw)[:, sp - WINDOW:].reshape(bp, WINDOW, kvh, hd)
    new_p = (last(kf), last(vf))
    g, q, kb, kf, vb, vf = _proj(xs, w_in, segs, TM_PROJ, "swa_proj_sample")
    og = _swa_attention(q.reshape(bs, ss, qw),
                        cache_k.reshape(bs, nprev, kvw).astype(BF16), kb.reshape(bs, ss, kvw),
                        cache_v.reshape(bs, nprev, kvw).astype(BF16), vb.reshape(bs, ss, kvw),
                        g.reshape(bs, ss, qw), sinks,
                        cq=ss, nch=1, nprev=nprev, prev_from_cur=False, name="swa_attn_sample")
    xs_new = _outproj_ln(og.reshape(bs * ss, qw), w_out, xs, ln_g, ln_b, alpha, TM_OUT, "swa_out_sample")
    new_s = (kf.reshape(bs, ss, kvh, hd), vf.reshape(bs, ss, kvh, hd))
    return xp_new, xs_new, new_p, new_s


def _s5_layer(xp, xs, bp, bs, state_re, state_im, w_in, a_re, a_im, log_dt, b_re, b_im, c_re, c_im,
              d_vec, w_glu, w_out, ln_g, ln_b, alpha):
    groups, p_ = a_re.shape
    c = groups * SSM_GROUP
    ns = c // LANES
    sg = SSM_SLAB_GROUPS
    L = SSM_L
    lw = L * LANES
    sp, ss = xp.shape[0] // bp, xs.shape[0] // bs
    assert sp % L == 0 and ss % L == 0
    w_in = w_in.astype(BF16)
    w_glu = w_glu.astype(BF16)
    w_out = w_out.astype(BF16)
    tables = _ssm_tables(a_re, a_im, log_dt, b_re, b_im, c_re, c_im, L)
    segs = ((c, c, 1.0, (("plain", BF16),), "silu"), (0, c, 1.0, (("rows", BF16),)))

    def split_state(hfin):
        re = hfin[..., :sg * p_].reshape(hfin.shape[:-1] + (sg, p_))
        im = hfin[..., sg * p_:].reshape(hfin.shape[:-1] + (sg, p_))
        return re, im

    g, u = _proj(xp, w_in, segs, TM_PROJ, "s5_proj_prompt")
    h0 = jnp.zeros((bp, ns, 1, 2 * sg * p_), F32)
    seqs_per_step = SUBLANES // (sg * p_ // LANES)
    if bp % seqs_per_step == 0:
        y, hfin = _ssm_dense(u.reshape(ns, bp, sp // L, lw), h0, tables,
                             tile_rows=SSM_TILE_ROWS, name="s5_ssm_prompt")
    else:
        y, hfin = _ssm(u.reshape(ns, bp, sp // L, lw), h0, tables,
                       nchains=1, tile_rows=SSM_TILE_ROWS, name="s5_ssm_prompt")
    xp_new = _s5_tail(y.reshape(ns, bp * sp // L, lw), u, g, d_vec, w_glu, w_out, xp, ln_g, ln_b, alpha,
                      TM_OUT, "s5_tail_prompt")
    re, im = split_state(hfin[:, :, 0])
    new_p = (re.reshape(bp, groups, p_), im.reshape(bp, groups, p_))

    g, u = _proj(xs, w_in, segs, TM_PROJ, "s5_proj_sample")
    nck = ss // L
    u_rows = u.reshape(ns, bs, nck, lw).transpose(0, 2, 1, 3).reshape(ns, 1, nck * bs, lw)
    st = jnp.concatenate([state_re.astype(F32).reshape(bs, ns, sg * p_),
                          state_im.astype(F32).reshape(bs, ns, sg * p_)], axis=-1)
    h0 = st.transpose(1, 0, 2)[None]
    y, hfin = _ssm(u_rows, h0, tables, nchains=bs, tile_rows=nck * bs, name="s5_ssm_sample")
    y = y.reshape(ns, nck, bs, lw).transpose(0, 2, 1, 3).reshape(ns, bs * nck, lw)
    xs_new = _s5_tail(y, u, g, d_vec, w_glu, w_out, xs, ln_g, ln_b, alpha, TM_OUT, "s5_tail_sample")
    re, im = split_state(hfin[0].transpose(1, 0, 2))
    new_s = (re.reshape(bs, groups, p_), im.reshape(bs, groups, p_))
    return xp_new, xs_new, new_p, new_s


def kernel(x_prompt, x_sample, cache_k_a, cache_v_a, cache_k_b, cache_v_b, state_re_c, state_im_c, ln_g, ln_b, w_in_a, w_out_a, w_in_b, sinks_b, w_out_b, w_in_c, a_re_c, a_im_c, log_dt_c, b_re_c, b_im_c, c_re_c, c_im_c, d_c, w_glu_c, w_out_c):
    depth = ln_g.shape[0]
    alpha = (2 * depth) ** 0.25
    bp, sp, d = x_prompt.shape
    bs, ss, _ = x_sample.shape
    xp = x_prompt.reshape(bp * sp, d)
    xs = x_sample.reshape(bs * ss, d)
    n_a = cache_k_a.shape[0]
    heads_a, hd_a = cache_k_a.shape[3], cache_k_a.shape[4]
    outs = {k: [] for k in ("ka_s", "va_s", "kb_p", "vb_p", "kb_s", "vb_s", "hr_p", "hi_p", "hr_s", "hi_s")}
    sb_stack = None
    for i in range(depth):
        j, kind = i // N_MIXERS, i % N_MIXERS
        if kind == 0:
            xp, xs, sb_stack, new_s = _sb_layer(xp, xs, bp, bs, cache_k_a, cache_v_a, w_in_a[j], w_out_a[j],
                                                ln_g[i], ln_b[i], alpha, (j, n_a, sb_stack))
            new_p, names = (), ("ka_s", "va_s")
        elif kind == 1:
            xp, xs, new_p, new_s = _swa_layer(xp, xs, bp, bs, cache_k_b[j], cache_v_b[j], w_in_b[j], sinks_b[j],
                                              w_out_b[j], ln_g[i], ln_b[i], alpha)
            names = ("kb_p", "vb_p", "kb_s", "vb_s")
        else:
            xp, xs, new_p, new_s = _s5_layer(xp, xs, bp, bs, state_re_c[j], state_im_c[j], w_in_c[j], a_re_c[j],
                                             a_im_c[j], log_dt_c[j], b_re_c[j], b_im_c[j], c_re_c[j], c_im_c[j],
                                             d_c[j], w_glu_c[j], w_out_c[j], ln_g[i], ln_b[i], alpha)
            names = ("hr_p", "hi_p", "hr_s", "hi_s")
        for nm, val in zip(names, tuple(new_p) + tuple(new_s)):
            outs[nm].append(val)
    ka_p = sb_stack[0].reshape(n_a, bp, sp, heads_a, hd_a)
    va_p = sb_stack[1].reshape(n_a, bp, sp, heads_a, hd_a)
    return (xp.reshape(bp, sp, d), xs.reshape(bs, ss, d),
            ka_p, va_p, jnp.stack(outs["ka_s"]), jnp.stack(outs["va_s"]),
            jnp.stack(outs["kb_p"]), jnp.stack(outs["vb_p"]), jnp.stack(outs["kb_s"]), jnp.stack(outs["vb_s"]),
            jnp.stack(outs["hr_p"]), jnp.stack(outs["hi_p"]), jnp.stack(outs["hr_s"]), jnp.stack(outs["hi_s"]))
```

```python
import functools
import math

import jax
import jax.numpy as jnp
from jax import lax
from jax.experimental import pallas as pl
from jax.experimental.pallas import tpu as pltpu

F32 = jnp.float32
BF16 = jnp.bfloat16

CHUNK = 64
WINDOW = 128
N_MIXERS = 3
SSM_GROUP = 16
LN_EPS = 1e-5
SB_KBLOCK = 128

LANES = 128
SUBLANES = 8
VMEM_LIMIT = 56 * 1024 * 1024

LOG2E = math.log2(math.e)
SB_DEAD_LOG2 = -127.0
SSM_L = SUBLANES
SSM_SLAB_GROUPS = LANES // SSM_GROUP


def _silu(x):
    return x * (1.0 / (1.0 + jnp.exp(-x)))


def _cparams(sem, vmem=VMEM_LIMIT):
    return pltpu.CompilerParams(dimension_semantics=sem, vmem_limit_bytes=vmem)


def _proj_kernel(*refs, segs, n_alias, step, stack_layer):
    x_ref, w_ref = refs[0], refs[1]
    out_refs = refs[2 + n_alias:]
    xb = x_ref[...].astype(BF16)
    outs = iter(out_refs)
    scr = out_refs[-1]
    for c0, width, scale, kinds in segs:
        o_refs = [next(outs) for _ in kinds]
        for s0 in range(0, width, step):
            s1 = min(width, s0 + step)
            acc = jnp.dot(xb, w_ref[:, c0 + s0:c0 + s1], preferred_element_type=F32)
            if scale != 1.0:
                acc = acc * scale
            for (kind, _), o_ref in zip(kinds, o_refs):
                if kind == "rows":
                    rows = acc.shape[0] // SUBLANES
                    for l0 in range(s0, s1, LANES):
                        slab = (l0 - s0) // LANES
                        scr[slab] = acc[:, l0 - s0:l0 - s0 + LANES]
                        for s in range(SUBLANES):
                            piece = scr[slab, pl.ds(s, rows, stride=SUBLANES), :]
                            o_ref[l0 // LANES, :, s * LANES:(s + 1) * LANES] = piece.astype(o_ref.dtype)
                elif kind == "stack_init":
                    for layer in range(o_ref.shape[0]):
                        val = acc.astype(o_ref.dtype) if layer == stack_layer else jnp.zeros(acc.shape, o_ref.dtype)
                        o_ref[layer, :, s0:s1] = val
                else:
                    o_ref[:, s0:s1] = acc.astype(o_ref.dtype)


def _resident_weight_spec(w, layer):
    if w.ndim == 3:
        return pl.BlockSpec((None,) + w.shape[1:], lambda i: (layer, 0, 0), pipeline_mode=pl.Buffered(1))
    return pl.BlockSpec(w.shape, lambda i: (0, 0), pipeline_mode=pl.Buffered(1))


def _proj(x2d, w_bf16, segs, tm, name, stack=None, w_layer=0):
    t, d = x2d.shape
    tm = min(tm, t)
    assert t % tm == 0 and tm % SUBLANES == 0
    step = 1024
    out_specs, out_shape, alias_in = [], [], []
    need_scratch = False
    for _, wd, _, kinds in segs:
        for kind, dt in kinds:
            if kind == "rows":
                need_scratch = True
                out_specs.append(pl.BlockSpec((wd // LANES, tm // SUBLANES, SUBLANES * LANES), lambda i: (0, i, 0)))
                out_shape.append(jax.ShapeDtypeStruct((wd // LANES, t // SUBLANES, SUBLANES * LANES), dt))
            elif kind == "stack_init":
                out_specs.append(pl.BlockSpec((stack[1], tm, wd), lambda i: (0, i, 0)))
                out_shape.append(jax.ShapeDtypeStruct((stack[1], t, wd), dt))
            elif kind == "stack":
                j, nl, prev = stack
                out_specs.append(pl.BlockSpec((None, tm, wd), functools.partial(lambda i, j_: (j_, i, 0), j_=j)))
                out_shape.append(jax.ShapeDtypeStruct((nl, t, wd), dt))
                alias_in.append((len(out_shape) - 1, prev[len(alias_in)]))
            else:
                out_specs.append(pl.BlockSpec((tm, wd), lambda i: (i, 0)))
                out_shape.append(jax.ShapeDtypeStruct((t, wd), dt))
    in_specs = [
        pl.BlockSpec((tm, d), lambda i: (i, 0)),
        _resident_weight_spec(w_bf16, w_layer),
    ] + [pl.BlockSpec(memory_space=pl.ANY) for _ in alias_in]
    scratch = [pltpu.VMEM((step // LANES, tm, LANES) if need_scratch else (1, SUBLANES, LANES), F32)]
    kern = functools.partial(_proj_kernel, segs=segs, n_alias=len(alias_in), step=step,
                             stack_layer=None if stack is None else stack[0])
    return pl.pallas_call(
        kern,
        grid=(t // tm,),
        in_specs=in_specs,
        out_specs=out_specs,
        out_shape=out_shape,
        scratch_shapes=scratch,
        input_output_aliases={2 + k: oi for k, (oi, _) in enumerate(alias_in)},
        compiler_params=_cparams(("parallel",)),
        name=name,
    )(x2d, w_bf16, *[buf for _, buf in alias_in])


def _layer_norm_rows(r, g, b):
    mu = jnp.mean(r, axis=-1, keepdims=True)
    rc = r - mu
    var = jnp.mean(rc * rc, axis=-1, keepdims=True)
    return rc * lax.rsqrt(var + LN_EPS) * g + b


def _outproj_kernel(a_ref, w_ref, x_ref, g_ref, b_ref, o_ref, *, alpha, sub):
    for r0 in range(0, a_ref.shape[0], sub):
        y = jnp.dot(a_ref[r0:r0 + sub, :], w_ref[...], preferred_element_type=F32)
        r = alpha * x_ref[r0:r0 + sub, :] + y
        o_ref[r0:r0 + sub, :] = _layer_norm_rows(r, g_ref[...], b_ref[...])


def _outproj_ln(a2d, w_bf16, x2d, ln_g, ln_b, alpha, tm, name, w_layer=0):
    t, wdt = a2d.shape
    d = x2d.shape[1]
    tm = min(tm, t)
    assert t % tm == 0
    return pl.pallas_call(
        functools.partial(_outproj_kernel, alpha=alpha, sub=min(tm, OUT_SUB_ROWS)),
        grid=(t // tm,),
        in_specs=[
            pl.BlockSpec((tm, wdt), lambda i: (i, 0)),
            _resident_weight_spec(w_bf16, w_layer),
            pl.BlockSpec((tm, d), lambda i: (i, 0)),
            pl.BlockSpec((1, d), lambda i: (0, 0)),
            pl.BlockSpec((1, d), lambda i: (0, 0)),
        ],
        out_specs=pl.BlockSpec((tm, d), lambda i: (i, 0)),
        out_shape=jax.ShapeDtypeStruct((t, d), F32),
        compiler_params=_cparams(("parallel",)),
        name=name,
    )(a2d, w_bf16, x2d, ln_g.reshape(1, d).astype(F32), ln_b.reshape(1, d).astype(F32))


def _scores(qs, kblks):
    return [lax.dot_general(q, k, (((1,), (1,)), ((), ())), preferred_element_type=F32) for q, k in zip(qs, kblks)]


def _sb_blocks(zs, vblks, masks, laters, accs, tri):
    n = len(zs)
    rows = zs[0].shape[0]
    lbetas, cats = [], []
    for c in range(n):
        z = zs[c]
        sp = jnp.maximum(z, 0.0) + jnp.log2(1.0 + jnp.exp2(-jnp.abs(z)))
        lbetas.append(z - sp)
        if masks[c] is not None:
            sp = jnp.where(masks[c], sp, 0.0)
        hi = sp.astype(BF16)
        lo = (sp - hi.astype(F32)).astype(BF16)
        cats.append(jnp.concatenate([hi, lo], axis=1))
    cs = jnp.dot(jnp.concatenate(cats, axis=0), tri, preferred_element_type=F32)
    new_l, ws = [], []
    for c in range(n):
        blk = cs[c * rows:(c + 1) * rows]
        w = jnp.exp2(lbetas[c] + blk[:, :SB_KBLOCK] + laters[c])
        if masks[c] is not None:
            w = jnp.where(masks[c], w, 0.0)
        ws.append(w.astype(BF16))
        new_l.append(laters[c] + blk[:, SB_KBLOCK:])
    new_a = [accs[c] + jnp.dot(ws[c], vblks[c], preferred_element_type=F32) for c in range(n)]
    return new_l, new_a


def _sb_alive(ls):
    m = ls[0]
    for l in ls[1:]:
        m = jnp.maximum(m, l)
    return jnp.max(m) > SB_DEAD_LOG2


def _sb_tri():
    j = jnp.arange(SB_KBLOCK)
    later = -(j[:, None] > j[None, :]).astype(BF16)
    ones = -jnp.ones((SB_KBLOCK, SB_KBLOCK), BF16)
    one = jnp.concatenate([later, ones], axis=1)
    return jnp.concatenate([one, one], axis=0)


def _sb_prompt_kernel(q_ref, k_ref, v_ref, g_ref, tri_ref, o_ref, *, nchain):
    kb = SB_KBLOCK
    i = pl.program_id(2)
    n0 = [i * nchain + c for c in range(nchain)]
    n0_max = n0[-1]
    col = lax.broadcasted_iota(jnp.int32, (kb, kb), 1)
    row = lax.broadcasted_iota(jnp.int32, (kb, kb), 0)

    def offs(nbs):
        return [pl.multiple_of(jnp.maximum(nb, 0) * kb, kb) for nb in nbs]

    def loadk(nbs):
        return [k_ref[0, pl.ds(o, kb), :] for o in offs(nbs)]

    def loadv(nbs):
        return [v_ref[0, pl.ds(o, kb), :] for o in offs(nbs)]

    def queries():
        return [q_ref[0, c * kb:(c + 1) * kb, :] for c in range(nchain)]

    zero = jnp.zeros((kb, kb), F32)
    diag = col < row
    laters, accs = _sb_blocks(_scores(queries(), loadk(n0)), loadv(n0), [diag] * nchain,
                              [zero] * nchain, [zero] * nchain, tri_ref[...])

    def cond(st):
        return jnp.logical_and(st[0] <= n0_max, st[1])

    def earlier_block(t, laters, accs):
        nbs = [n0[c] - t for c in range(nchain)]
        ls = [laters[c] + jnp.where(nbs[c] >= 0, 0.0, -1e30).astype(F32) for c in range(nchain)]
        return _sb_blocks(_scores(queries(), loadk(nbs)), loadv(nbs), [None] * nchain, ls, list(accs), tri_ref[...])

    laters, accs = earlier_block(1, laters, accs)

    def body(st):
        ls, as_ = earlier_block(st[0], st[2], st[3])
        return (st[0] + 1, _sb_alive(ls), tuple(ls), tuple(as_))

    st = lax.while_loop(cond, body, (jnp.int32(2), _sb_alive(laters), tuple(laters), tuple(accs)))
    accs = st[3]
    for c in range(nchain):
        g = g_ref[0, c * kb:(c + 1) * kb, :].astype(F32)
        o_ref[0, c * kb:(c + 1) * kb, :] = (accs[c] * _silu(g)).astype(o_ref.dtype)


def _sb_attention_prompt(q, k, v, g, heads, nchain, name):
    b, s, w = q.shape
    hd = w // heads
    tq = SB_KBLOCK * nchain
    assert hd == SB_KBLOCK and s % tq == 0
    qmap = lambda bi, h, i: (bi, i, h)
    kmap = lambda bi, h, i: (bi, 0, h)
    return pl.pallas_call(
        functools.partial(_sb_prompt_kernel, nchain=nchain),
        grid=(b, heads, s // tq),
        in_specs=[
            pl.BlockSpec((1, tq, hd), qmap),
            pl.BlockSpec((1, s, hd), kmap),
            pl.BlockSpec((1, s, hd), kmap),
            pl.BlockSpec((1, tq, hd), qmap),
            pl.BlockSpec((2 * SB_KBLOCK, 2 * SB_KBLOCK), lambda bi, h, i: (0, 0)),
        ],
        out_specs=pl.BlockSpec((1, tq, hd), qmap),
        out_shape=jax.ShapeDtypeStruct((b, s, w), BF16),
        compiler_params=_cparams(("parallel", "parallel", "arbitrary")),
        name=name,
    )(q, k, v, g, _sb_tri())


def _sb_sample_kernel(q_ref, kn_ref, vn_ref, g_ref, ck_ref, cv_ref, ckh_ref, cvh_ref, tri_ref, o_ref,
                      kscr, vscr, sem, *, heads, wblk, nrem, layer):
    kb = SB_KBLOCK
    b = pl.program_id(0)
    ss = q_ref.shape[1]
    qs = [q_ref[0, :, h * kb:(h + 1) * kb] for h in range(heads)]
    zpad = jnp.zeros((kb - ss, kb), BF16)
    col = lax.broadcasted_iota(jnp.int32, (ss, kb), 1)
    row = lax.broadcasted_iota(jnp.int32, (ss, kb), 0)
    new_mask = col < row
    kbl = [jnp.concatenate([kn_ref[0, :, h * kb:(h + 1) * kb], zpad], axis=0) for h in range(heads)]
    vbl = [jnp.concatenate([vn_ref[0, :, h * kb:(h + 1) * kb], zpad], axis=0) for h in range(heads)]
    zero = jnp.zeros((ss, kb), F32)
    laters, accs = _sb_blocks(_scores(qs, kbl), vbl, [new_mask] * heads, [zero] * heads, [zero] * heads, tri_ref[...])

    def head_blocks(ref, r0):
        return [ref[pl.ds(r0 + h, kb, stride=heads), :].astype(BF16) for h in range(heads)]

    for j in range(wblk):
        r0 = (wblk - 1 - j) * kb * heads
        laters, accs = _sb_blocks(_scores(qs, head_blocks(ck_ref.at[0], r0)), head_blocks(cv_ref.at[0], r0),
                                  [None] * heads, laters, accs, tri_ref[...])

    if nrem > 0:
        def cond(st):
            return jnp.logical_and(st[0] < nrem, st[1])

        def body(st):
            t = st[0]
            r0 = pl.multiple_of((nrem - 1 - t) * (kb * heads), kb * heads)
            ck = pltpu.make_async_copy(ckh_ref.at[layer, b, pl.ds(r0, kb * heads)], kscr, sem.at[0])
            cv = pltpu.make_async_copy(cvh_ref.at[layer, b, pl.ds(r0, kb * heads)], vscr, sem.at[1])
            ck.start()
            cv.start()
            ck.wait()
            cv.wait()
            ls, as_ = _sb_blocks(_scores(qs, head_blocks(kscr, 0)), head_blocks(vscr, 0), [None] * heads,
                                 list(st[2]), list(st[3]), tri_ref[...])
            return (t + 1, _sb_alive(ls), tuple(ls), tuple(as_))

        st = lax.while_loop(cond, body, (jnp.int32(0), _sb_alive(laters), tuple(laters), tuple(accs)))
        accs = st[3]
    for h in range(heads):
        g = g_ref[0, :, h * kb:(h + 1) * kb].astype(F32)
        o_ref[0, :, h * kb:(h + 1) * kb] = (accs[h] * _silu(g)).astype(o_ref.dtype)


def _sb_attention_sample(q, k_new, v_new, g, cache_k, cache_v, layer, wblk, name):
    b, ss, w = q.shape
    nl, _, past, heads, hd = cache_k.shape
    assert hd == SB_KBLOCK and w == heads * hd and ss <= SB_KBLOCK and ss % 16 == 0
    wblk = min(wblk, past // SB_KBLOCK)
    tail = wblk * SB_KBLOCK
    assert past % tail == 0
    nrem = past // SB_KBLOCK - wblk
    ck = cache_k.reshape(nl, b, past * heads, hd)
    cv = cache_v.reshape(nl, b, past * heads, hd)
    tok = lambda bi: (bi, 0, 0)
    tail_map = lambda bi: (layer, bi, past // tail - 1, 0)
    kern = functools.partial(_sb_sample_kernel, heads=heads, wblk=wblk, nrem=nrem, layer=layer)
    return pl.pallas_call(
        kern,
        grid=(b,),
        in_specs=[
            pl.BlockSpec((1, ss, w), tok),
            pl.BlockSpec((1, ss, w), tok),
            pl.BlockSpec((1, ss, w), tok),
            pl.BlockSpec((1, ss, w), tok),
            pl.BlockSpec((None, 1, tail * heads, hd), tail_map),
            pl.BlockSpec((None, 1, tail * heads, hd), tail_map),
            pl.BlockSpec(memory_space=pl.ANY),
            pl.BlockSpec(memory_space=pl.ANY),
            pl.BlockSpec((2 * SB_KBLOCK, 2 * SB_KBLOCK), lambda bi: (0, 0)),
        ],
        out_specs=pl.BlockSpec((1, ss, w), tok),
        out_shape=jax.ShapeDtypeStruct((b, ss, w), BF16),
        scratch_shapes=[pltpu.VMEM((SB_KBLOCK * heads, hd), F32), pltpu.VMEM((SB_KBLOCK * heads, hd), F32),
                        pltpu.SemaphoreType.DMA((2,))],
        compiler_params=_cparams(("arbitrary",)),
        name=name,
    )(q, k_new, v_new, g, ck, cv, ck, cv, _sb_tri())


def _swa_kernel(q_ref, kp_ref, kc_ref, vp_ref, vc_ref, g_ref, bias_ref, sink_ref, o_ref,
                *, cq, nch, span, kvh, rep, hd, mask_first):
    i = pl.program_id(1)
    kbuf = jnp.concatenate([kp_ref[0], kc_ref[0]], axis=0)
    vbuf = jnp.concatenate([vp_ref[0], vc_ref[0]], axis=0)
    nprev = kp_ref.shape[1]
    gw = rep * hd
    for kv in range(kvh):
        kk = kbuf[:, kv * hd:(kv + 1) * hd]
        vv = vbuf[:, kv * hd:(kv + 1) * hd]
        bias = bias_ref[kv]
        sink = sink_ref[kv]
        sts = []
        for j in range(nch):
            ks = kk[j * cq:j * cq + span]
            qs = q_ref[0, j * cq:(j + 1) * cq, kv * gw:(kv + 1) * gw]
            lhs = jnp.concatenate([qs[:, r * hd:(r + 1) * hd] for r in range(rep)], axis=0)
            st = lax.dot_general(ks, lhs, (((1,), (1,)), ((), ())), preferred_element_type=F32) + bias
            if mask_first and j * cq < nprev:
                buf_row = j * cq + lax.broadcasted_iota(jnp.int32, st.shape, 0)
                valid = jnp.logical_or(i > 0, buf_row >= nprev)
                st = jnp.where(valid, st, -jnp.inf)
            sts.append(st)
        ps = []
        for st in sts:
            m = jnp.maximum(jnp.max(st, axis=0, keepdims=True), sink)
            p = jnp.exp2(st - m)
            den = jnp.sum(p, axis=0, keepdims=True) + jnp.exp2(sink - m)
            ps.append((p * (1.0 / den)).astype(BF16))
        for j in range(nch):
            vs = vv[j * cq:j * cq + span]
            o = lax.dot_general(ps[j], vs, (((0,), (0,)), ((), ())), preferred_element_type=F32)
            ot = jnp.concatenate([o[r * cq:(r + 1) * cq] for r in range(rep)], axis=1)
            g = g_ref[0, j * cq:(j + 1) * cq, kv * gw:(kv + 1) * gw].astype(F32)
            o_ref[0, j * cq:(j + 1) * cq, kv * gw:(kv + 1) * gw] = (ot * _silu(g)).astype(o_ref.dtype)


def _alibi_bias_t(kvh, rep, cq, span, nprev):
    n = kvh * rep
    slopes = 2.0 ** (-8.0 * jnp.arange(1, n + 1, dtype=F32) / n)
    qi = jnp.arange(cq, dtype=jnp.int32)
    kj = jnp.arange(span, dtype=jnp.int32)
    dist = jnp.abs(qi[None, :] + nprev - kj[:, None]).astype(F32)
    bias = -(slopes * LOG2E).reshape(kvh, 1, rep, 1) * dist[None, :, None, :]
    return bias.reshape(kvh, span, rep * cq)


def _swa_attention(q, k_prev_src, k_cur, v_prev_src, v_cur, g, sinks, *, cq, nch, nprev, prev_from_cur, name):
    b, sq, qw = q.shape
    kvw = k_cur.shape[2]
    kvh, rep = sinks.shape
    hd = kvw // kvh
    rows = cq * nch
    span = nprev + cq
    assert sq % rows == 0 and qw == kvh * rep * hd
    if prev_from_cur:
        assert rows % nprev == 0
        ratio = rows // nprev
        prev_map = lambda bi, i: (bi, jnp.maximum(i * ratio - 1, 0), 0)
    else:
        assert sq == rows
        prev_map = lambda bi, i: (bi, 0, 0)
    cur_map = lambda bi, i: (bi, i, 0)
    const3 = lambda bi, i: (0, 0, 0)
    sink_t = jnp.broadcast_to((sinks.astype(F32) * LOG2E)[:, None, :, None], (kvh, 1, rep, cq)).reshape(kvh, 1, rep * cq)
    kern = functools.partial(_swa_kernel, cq=cq, nch=nch, span=span, kvh=kvh, rep=rep, hd=hd,
                             mask_first=prev_from_cur)
    return pl.pallas_call(
        kern,
        grid=(b, sq // rows),
        in_specs=[
            pl.BlockSpec((1, rows, qw), cur_map),
            pl.BlockSpec((1, nprev, kvw), prev_map),
            pl.BlockSpec((1, rows, kvw), cur_map),
            pl.BlockSpec((1, nprev, kvw), prev_map),
            pl.BlockSpec((1, rows, kvw), cur_map),
            pl.BlockSpec((1, rows, qw), cur_map),
            pl.BlockSpec((kvh, span, rep * cq), const3),
            pl.BlockSpec((kvh, 1, rep * cq), const3),
        ],
        out_specs=pl.BlockSpec((1, rows, qw), cur_map),
        out_shape=jax.ShapeDtypeStruct((b, sq, qw), BF16),
        compiler_params=_cparams(("parallel", "arbitrary")),
        name=name,
    )(q, k_prev_src, k_cur, v_prev_src, v_cur, g, _alibi_bias_t(kvh, rep, cq, span, nprev), sink_t)


def _ssm_kernel(u_ref, w_ref, m_ref, v_ref, lam_ref, h0_ref, y_ref, hfin_ref, s_scr, h_scr, carry, *, nchains):
    ti = pl.program_id(2)
    nrow = s_scr.shape[0]
    half = s_scr.shape[1] // 2
    steps = nrow // nchains

    @pl.when(ti == 0)
    def _():
        carry[...] = h0_ref[0, 0]

    u = u_ref[0, 0]
    s_scr[...] = jnp.dot(u, w_ref[0], preferred_element_type=F32)

    lam = lam_ref[0]
    lr = jnp.broadcast_to(lam[:, :half], (nchains, half))
    li = jnp.broadcast_to(lam[:, half:], (nchains, half))

    def step(c, h):
        r0 = pl.multiple_of(c * nchains, nchains)
        h_scr[pl.ds(r0, nchains), :] = h
        s = s_scr[pl.ds(r0, nchains), :]
        hr, hi = h[:, :half], h[:, half:]
        nr = lr * hr - li * hi + s[:, :half]
        ni = lr * hi + li * hr + s[:, half:]
        return jnp.concatenate([nr, ni], axis=1)

    h = lax.fori_loop(0, steps, step, carry[...], unroll=(8 if steps % 8 == 0 else 1))
    carry[...] = h
    hfin_ref[0, 0] = h

    y_ref[0, 0] = (jnp.dot(u, m_ref[0], preferred_element_type=F32)
                   + jnp.dot(h_scr[...].astype(BF16), v_ref[0], preferred_element_type=F32))


def _ssm_dense_kernel(u_ref, w_ref, m_ref, v_ref, lam_ref, h0_ref, y_ref, hfin_ref,
                      sre, sim, hre, him, cre, cim, *, nb, kt):
    ti = pl.program_id(2)
    rows = u_ref.shape[2]
    half = kt * LANES

    def fold(row_of):
        return jnp.concatenate([row_of(b, k) for b in range(nb) for k in range(kt)], axis=0)

    @pl.when(ti == 0)
    def _():
        cre[...] = fold(lambda b, k: h0_ref[b, 0, :, k * LANES:(k + 1) * LANES])
        cim[...] = fold(lambda b, k: h0_ref[b, 0, :, half + k * LANES:half + (k + 1) * LANES])

    us = [u_ref[0, b] for b in range(nb)]
    for b in range(nb):
        s = jnp.dot(us[b], w_ref[0], preferred_element_type=F32)
        for k in range(kt):
            sre[pl.ds(b * kt + k, rows, stride=SUBLANES), :] = s[:, k * LANES:(k + 1) * LANES]
            sim[pl.ds(b * kt + k, rows, stride=SUBLANES), :] = s[:, half + k * LANES:half + (k + 1) * LANES]

    lam = lam_ref[0]
    lr = fold(lambda b, k: lam[:, k * LANES:(k + 1) * LANES])
    li = fold(lambda b, k: lam[:, half + k * LANES:half + (k + 1) * LANES])

    def step(c, carry):
        hr, hi = carry
        r0 = pl.multiple_of(c * SUBLANES, SUBLANES)
        hre[pl.ds(r0, SUBLANES), :] = hr
        him[pl.ds(r0, SUBLANES), :] = hi
        nr = lr * hr - li * hi + sre[pl.ds(r0, SUBLANES), :]
        ni = lr * hi + li * hr + sim[pl.ds(r0, SUBLANES), :]
        return nr, ni

    hr, hi = lax.fori_loop(0, rows, step, (cre[...], cim[...]), unroll=(8 if rows % 8 == 0 else 1))
    cre[...] = hr
    cim[...] = hi
    for b in range(nb):
        pieces = ([hr[b * kt + k:b * kt + k + 1] for k in range(kt)]
                  + [hi[b * kt + k:b * kt + k + 1] for k in range(kt)])
        hfin_ref[b, 0] = jnp.concatenate(pieces, axis=1)
        hcat = jnp.concatenate(
            [hre[pl.ds(b * kt + k, rows, stride=SUBLANES), :] for k in range(kt)]
            + [him[pl.ds(b * kt + k, rows, stride=SUBLANES), :] for k in range(kt)], axis=1)
        y_ref[0, b] = (jnp.dot(us[b], m_ref[0], preferred_element_type=F32)
                       + jnp.dot(hcat.astype(BF16), v_ref[0], preferred_element_type=F32))


def _ssm_dense(u_rows, h0, tables, *, tile_rows, name):
    w, m, v, lam = tables
    ns, b, r, lw = u_rows.shape
    sw = w.shape[2]
    kt = sw // (2 * LANES)
    nb = SUBLANES // kt
    tile_rows = min(tile_rows, r)
    assert r % tile_rows == 0 and b % nb == 0 and nb * kt == SUBLANES
    row_map = lambda o, bi, ti: (o, bi, ti, 0)
    tab = lambda o, bi, ti: (o, 0, 0)
    st_map = lambda o, bi, ti: (bi, o, 0, 0)
    fold_scr = pltpu.VMEM((tile_rows * SUBLANES, LANES), F32)
    return pl.pallas_call(
        functools.partial(_ssm_dense_kernel, nb=nb, kt=kt),
        grid=(ns, b // nb, r // tile_rows),
        in_specs=[
            pl.BlockSpec((1, nb, tile_rows, lw), row_map),
            pl.BlockSpec((1, lw, sw), tab),
            pl.BlockSpec((1, lw, lw), tab),
            pl.BlockSpec((1, sw, lw), tab),
            pl.BlockSpec((1, 1, sw), tab),
            pl.BlockSpec((nb, 1, 1, sw), st_map),
        ],
        out_specs=[
            pl.BlockSpec((1, nb, tile_rows, lw), row_map),
            pl.BlockSpec((nb, 1, 1, sw), st_map),
        ],
        out_shape=[jax.ShapeDtypeStruct((ns, b, r, lw), F32),
                   jax.ShapeDtypeStruct((b, ns, 1, sw), F32)],
        scratch_shapes=[fold_scr, fold_scr, fold_scr, fold_scr,
                        pltpu.VMEM((SUBLANES, LANES), F32), pltpu.VMEM((SUBLANES, LANES), F32)],
        compiler_params=_cparams(("parallel", "parallel", "arbitrary")),
        name=name,
    )(u_rows, w, m, v, lam, h0)


def _ssm_tables(a_re, a_im, log_dt, b_re, b_im, c_re, c_im, L):
    g_, p_ = a_re.shape
    sg = SSM_SLAB_GROUPS
    ns = g_ // sg
    hp = lax.Precision.HIGHEST
    a_re = jnp.minimum(a_re.astype(F32), -1e-4)
    a_im = a_im.astype(F32)
    dt = jnp.exp(log_dt.astype(F32))[:, None]
    mag = jnp.exp(dt * a_re)
    lb_re = mag * jnp.cos(dt * a_im)
    lb_im = mag * jnp.sin(dt * a_im)
    n_re = lb_re - 1.0
    den = a_re * a_re + a_im * a_im
    f_re = ((n_re * a_re + lb_im * a_im) / den)[..., None]
    f_im = ((lb_im * a_re - n_re * a_im) / den)[..., None]
    br, bi = b_re.astype(F32), b_im.astype(F32)
    bb_re = f_re * br - f_im * bi
    bb_im = f_re * bi + f_im * br
    cr, ci = c_re.astype(F32), c_im.astype(F32)
    tau = jnp.arange(L + 1, dtype=F32)[:, None, None]
    pmag = jnp.exp(tau * dt * a_re)
    pw_re = pmag * jnp.cos(tau * dt * a_im)
    pw_im = pmag * jnp.sin(tau * dt * a_im)

    lb_r = pw_re[:, :, :, None] * bb_re[None] - pw_im[:, :, :, None] * bb_im[None]
    lb_i = pw_re[:, :, :, None] * bb_im[None] + pw_im[:, :, :, None] * bb_re[None]
    kt = (jnp.einsum('gop,tgpi->tgoi', cr, lb_r[:L], precision=hp)
          - jnp.einsum('gop,tgpi->tgoi', ci, lb_i[:L], precision=hp))
    lag = jnp.arange(L)[None, :] - jnp.arange(L)[:, None]
    kst = jnp.where((lag >= 0)[:, :, None, None, None], kt[jnp.clip(lag, 0, L - 1)], 0.0)

    rows_u = L * LANES
    cols_s = 2 * sg * p_
    rid = jnp.arange(rows_u)
    grp_u = (rid % LANES) // SSM_GROUP
    sid = jnp.arange(cols_s)
    grp_s = (sid % (sg * p_)) // p_

    def expand(compact, rep_mat, grp_r, grp_c):
        full = jnp.einsum('nrc,ck->nrk', compact.astype(BF16), rep_mat.astype(BF16),
                          preferred_element_type=BF16)
        return jnp.where(grp_r[:, None] == grp_c[None, :], full, jnp.zeros((), BF16))

    m_c = kst.reshape(L, L, ns, sg, SSM_GROUP, SSM_GROUP).transpose(2, 0, 3, 5, 1, 4).reshape(ns, rows_u, L * SSM_GROUP)
    cid = jnp.arange(rows_u)
    rep_to = ((cid[None, :] // LANES == jnp.arange(L * SSM_GROUP)[:, None] // SSM_GROUP)
              & (cid[None, :] % SSM_GROUP == jnp.arange(L * SSM_GROUP)[:, None] % SSM_GROUP)).astype(F32)
    m = expand(m_c, rep_to, grp_u, grp_u)
    wri = jnp.stack([lb_r[:L][::-1], lb_i[:L][::-1]], axis=0).reshape(2, L, ns, sg, p_, SSM_GROUP)
    w_c = wri.transpose(2, 1, 3, 5, 0, 4).reshape(ns, rows_u, 2 * p_)
    rep_ap = ((sid[None, :] // (sg * p_) == jnp.arange(2 * p_)[:, None] // p_)
              & (sid[None, :] % p_ == jnp.arange(2 * p_)[:, None] % p_)).astype(F32)
    w = expand(w_c, rep_ap, grp_u, grp_s)
    cl_r = cr[None] * pw_re[1:][:, :, None, :] - ci[None] * pw_im[1:][:, :, None, :]
    cl_i = cr[None] * pw_im[1:][:, :, None, :] + ci[None] * pw_re[1:][:, :, None, :]
    vri = jnp.stack([cl_r, -cl_i], axis=0).reshape(2, L, ns, sg, SSM_GROUP, p_)
    v_c = vri.transpose(2, 0, 3, 5, 1, 4).reshape(ns, cols_s, L * SSM_GROUP)
    v = expand(v_c, rep_to, grp_s, grp_u)
    lam = jnp.stack([pw_re[L], pw_im[L]], axis=0).reshape(2, ns, sg * p_)
    lam = jnp.transpose(lam, (1, 0, 2)).reshape(ns, 1, cols_s)
    return w, m, v, lam


def _ssm(u_rows, h0, tables, *, nchains, tile_rows, name):
    w, m, v, lam = tables
    ns, b, r, lw = u_rows.shape
    sw = w.shape[2]
    tile_rows = min(tile_rows, r)
    assert r % tile_rows == 0 and tile_rows % nchains == 0
    row_map = lambda o, bi, ti: (o, bi, ti, 0)
    tab = lambda o, bi, ti: (o, 0, 0)
    st_map = lambda o, bi, ti: (bi, o, 0, 0)
    return pl.pallas_call(
        functools.partial(_ssm_kernel, nchains=nchains),
        grid=(ns, b, r // tile_rows),
        in_specs=[
            pl.BlockSpec((1, 1, tile_rows, lw), row_map),
            pl.BlockSpec((1, lw, sw), tab),
            pl.BlockSpec((1, lw, lw), tab),
            pl.BlockSpec((1, sw, lw), tab),
            pl.BlockSpec((1, 1, sw), tab),
            pl.BlockSpec((1, 1, nchains, sw), st_map),
        ],
        out_specs=[
            pl.BlockSpec((1, 1, tile_rows, lw), row_map),
            pl.BlockSpec((1, 1, nchains, sw), st_map),
        ],
        out_shape=[jax.ShapeDtypeStruct((ns, b, r, lw), F32),
                   jax.ShapeDtypeStruct((b, ns, nchains, sw), F32)],
        scratch_shapes=[pltpu.VMEM((tile_rows, sw), F32), pltpu.VMEM((tile_rows, sw), F32),
                        pltpu.VMEM((nchains, sw), F32)],
        compiler_params=_cparams(("parallel", "parallel", "arbitrary")),
        name=name,
    )(u_rows, w, m, v, lam, h0)


def _gelu_tanh(x):
    return 0.5 * x * (1.0 + jnp.tanh(math.sqrt(2.0 / math.pi) * (x + 0.044715 * (x * x * x))))


def _s5_tail_kernel(y_ref, u_ref, g_ref, d_ref, wg_ref, wo_ref, x_ref, lg_ref, lb_ref, o_ref, y_scr, u_scr, *, alpha):
    ns, rows, _ = y_ref.shape
    for o in range(ns):
        for s in range(SUBLANES):
            y_scr[o, pl.ds(s, rows, stride=SUBLANES), :] = y_ref[o, :, s * LANES:(s + 1) * LANES]
            u_scr[o, pl.ds(s, rows, stride=SUBLANES), :] = u_ref[o, :, s * LANES:(s + 1) * LANES].astype(F32)
    y_tok = jnp.concatenate([y_scr[o] for o in range(ns)], axis=1)
    u_tok = jnp.concatenate([u_scr[o] for o in range(ns)], axis=1)
    y = _gelu_tanh(y_tok + d_ref[...] * u_tok)
    z = jnp.dot(y.astype(BF16), wg_ref[...], preferred_element_type=F32)
    y = y * (1.0 / (1.0 + jnp.exp(-z)))
    a = (y * _silu(g_ref[...].astype(F32))).astype(BF16)
    out = jnp.dot(a, wo_ref[...], preferred_element_type=F32)
    r = alpha * x_ref[...] + out
    o_ref[...] = _layer_norm_rows(r, lg_ref[...], lb_ref[...])


def _s5_tail(y_rows, u_rows, g2d, d_vec, w_glu, w_out, x2d, ln_g, ln_b, alpha, tm, name):
    ns, tr, lw = y_rows.shape
    t = tr * SUBLANES
    c = ns * LANES
    d = x2d.shape[1]
    tm = min(tm, t)
    assert t % tm == 0 and tm % SUBLANES == 0 and lw == SUBLANES * LANES
    row_spec = pl.BlockSpec((ns, tm // SUBLANES, lw), lambda i: (0, i, 0))
    const = lambda i: (0, 0)
    return pl.pallas_call(
        functools.partial(_s5_tail_kernel, alpha=alpha),
        grid=(t // tm,),
        in_specs=[
            row_spec, row_spec,
            pl.BlockSpec((tm, c), lambda i: (i, 0)),
            pl.BlockSpec((1, c), const),
            pl.BlockSpec((c, c), const, pipeline_mode=pl.Buffered(1)),
            pl.BlockSpec((c, d), const, pipeline_mode=pl.Buffered(1)),
            pl.BlockSpec((tm, d), lambda i: (i, 0)),
            pl.BlockSpec((1, d), const),
            pl.BlockSpec((1, d), const),
        ],
        out_specs=pl.BlockSpec((tm, d), lambda i: (i, 0)),
        out_shape=jax.ShapeDtypeStruct((t, d), F32),
        scratch_shapes=[pltpu.VMEM((ns, tm, LANES), F32), pltpu.VMEM((ns, tm, LANES), F32)],
        compiler_params=_cparams(("parallel",)),
        name=name,
    )(y_rows, u_rows, g2d, d_vec.reshape(1, c).astype(F32), w_glu, w_out, x2d,
      ln_g.reshape(1, d).astype(F32), ln_b.reshape(1, d).astype(F32))


TM_PROJ = 512
TM_OUT = 512
OUT_SUB_ROWS = 256
SB_CHAINS = 16
SB_SAMPLE_TAIL_BLOCKS = 2
SWA_CHUNKS_PER_STEP = 4
SSM_TILE_ROWS = 512


def _sb_layer(xp, xs, bp, bs, cache_k, cache_v, w_in, w_out, ln_g, ln_b, alpha, stack_p):
    layer = stack_p[0]
    heads, hd = cache_k.shape[3], cache_k.shape[4]
    width = heads * hd
    sp, ss = xp.shape[0] // bp, xs.shape[0] // bs
    w_in = w_in.astype(BF16)
    w_out = w_out.astype(BF16)

    def segs(kv_kind):
        return ((3 * width, width, 1.0, (("plain", BF16),)),
                (0, width, LOG2E * hd ** -0.5, (("plain", BF16),)),
                (width, width, 1.0, (("plain", BF16), (kv_kind, F32))),
                (2 * width, width, 1.0, (("plain", BF16), (kv_kind, F32))))

    r3 = lambda a, b_, s_: a.reshape(b_, s_, width)
    kv_kind = "stack_init" if stack_p[2] is None else "stack"
    g, q, kb, kf, vb, vf = _proj(xp, w_in, segs(kv_kind), TM_PROJ, "sb_proj_prompt", stack=stack_p, w_layer=layer)
    og = _sb_attention_prompt(r3(q, bp, sp), r3(kb, bp, sp), r3(vb, bp, sp), r3(g, bp, sp), heads,
                              nchain=min(SB_CHAINS, sp // SB_KBLOCK), name="sb_attn_prompt")
    xp_new = _outproj_ln(og.reshape(bp * sp, width), w_out, xp, ln_g, ln_b, alpha, TM_OUT, "sb_out_prompt",
                         w_layer=layer)
    g, q, kb, kf_s, vb, vf_s = _proj(xs, w_in, segs("plain"), TM_PROJ, "sb_proj_sample", w_layer=layer)
    og = _sb_attention_sample(r3(q, bs, ss), r3(kb, bs, ss), r3(vb, bs, ss), r3(g, bs, ss), cache_k, cache_v,
                              layer, SB_SAMPLE_TAIL_BLOCKS, "sb_attn_sample")
    xs_new = _outproj_ln(og.reshape(bs * ss, width), w_out, xs, ln_g, ln_b, alpha, TM_OUT, "sb_out_sample",
                         w_layer=layer)
    new_s = (kf_s.reshape(bs, ss, heads, hd), vf_s.reshape(bs, ss, heads, hd))
    return xp_new, xs_new, (kf, vf), new_s


def _swa_layer(xp, xs, bp, bs, cache_k, cache_v, w_in, sinks, w_out, ln_g, ln_b, alpha):
    kvh, hd = cache_k.shape[2], cache_k.shape[3]
    rep = sinks.shape[1]
    qw, kvw = kvh * rep * hd, kvh * hd
    nprev = cache_k.shape[1]
    sp, ss = xp.shape[0] // bp, xs.shape[0] // bs
    assert nprev == WINDOW and sp >= WINDOW
    w_in = w_in.astype(BF16)
    w_out = w_out.astype(BF16)
    segs = ((qw + 2 * kvw, qw, 1.0, (("plain", BF16),)),
            (0, qw, LOG2E * hd ** -0.5, (("plain", BF16),)),
            (qw, kvw, 1.0, (("plain", BF16), ("plain", F32))),
            (qw + kvw, kvw, 1.0, (("plain", BF16), ("plain", F32))))
    g, q, kb, kf, vb, vf = _proj(xp, w_in, segs, TM_PROJ, "swa_proj_prompt")
    k3, v3 = kb.reshape(bp, sp, kvw), vb.reshape(bp, sp, kvw)
    og = _swa_attention(q.reshape(bp, sp, qw), k3, k3, v3, v3, g.reshape(bp, sp, qw), sinks,
                        cq=CHUNK, nch=SWA_CHUNKS_PER_STEP, nprev=WINDOW, prev_from_cur=True,
                        name="swa_attn_prompt")
    xp_new = _outproj_ln(og.reshape(bp * sp, qw), w_out, xp, ln_g, ln_b, alpha, TM_OUT, "swa_out_prompt")
    last = lambda a: a.reshape(bp, sp, kvw)[:, sp - WINDOW:].reshape(bp, WINDOW, kvh, hd)
    new_p = (last(kf), last(vf))
    g, q, kb, kf, vb, vf = _proj(xs, w_in, segs, TM_PROJ, "swa_proj_sample")
    og = _swa_attention(q.reshape(bs, ss, qw),
                        cache_k.reshape(bs, nprev, kvw).astype(BF16), kb.reshape(bs, ss, kvw),
                        cache_v.reshape(bs, nprev, kvw).astype(BF16), vb.reshape(bs, ss, kvw),
                        g.reshape(bs, ss, qw), sinks,
                        cq=ss, nch=1, nprev=nprev, prev_from_cur=False, name="swa_attn_sample")
    xs_new = _outproj_ln(og.reshape(bs * ss, qw), w_out, xs, ln_g, ln_b, alpha, TM_OUT, "swa_out_sample")
    new_s = (kf.reshape(bs, ss, kvh, hd), vf.reshape(bs, ss, kvh, hd))
    return xp_new, xs_new, new_p, new_s


def _s5_layer(xp, xs, bp, bs, state_re, state_im, w_in, a_re, a_im, log_dt, b_re, b_im, c_re, c_im,
              d_vec, w_glu, w_out, ln_g, ln_b, alpha):
    groups, p_ = a_re.shape
    c = groups * SSM_GROUP
    ns = c // LANES
    sg = SSM_SLAB_GROUPS
    L = SSM_L
    lw = L * LANES
    sp, ss = xp.shape[0] // bp, xs.shape[0] // bs
    assert sp % L == 0 and ss % L == 0
    w_in = w_in.astype(BF16)
    w_glu = w_glu.astype(BF16)
    w_out = w_out.astype(BF16)
    tables = _ssm_tables(a_re, a_im, log_dt, b_re, b_im, c_re, c_im, L)
    segs = ((c, c, 1.0, (("plain", BF16),)), (0, c, 1.0, (("rows", BF16),)))

    def split_state(hfin):
        re = hfin[..., :sg * p_].reshape(hfin.shape[:-1] + (sg, p_))
        im = hfin[..., sg * p_:].reshape(hfin.shape[:-1] + (sg, p_))
        return re, im

    g, u = _proj(xp, w_in, segs, TM_PROJ, "s5_proj_prompt")
    h0 = jnp.zeros((bp, ns, 1, 2 * sg * p_), F32)
    seqs_per_step = SUBLANES // (sg * p_ // LANES)
    if bp % seqs_per_step == 0:
        y, hfin = _ssm_dense(u.reshape(ns, bp, sp // L, lw), h0, tables,
                             tile_rows=SSM_TILE_ROWS, name="s5_ssm_prompt")
    else:
        y, hfin = _ssm(u.reshape(ns, bp, sp // L, lw), h0, tables,
                       nchains=1, tile_rows=SSM_TILE_ROWS, name="s5_ssm_prompt")
    xp_new = _s5_tail(y.reshape(ns, bp * sp // L, lw), u, g, d_vec, w_glu, w_out, xp, ln_g, ln_b, alpha,
                      TM_OUT, "s5_tail_prompt")
    re, im = split_state(hfin[:, :, 0])
    new_p = (re.reshape(bp, groups, p_), im.reshape(bp, groups, p_))

    g, u = _proj(xs, w_in, segs, TM_PROJ, "s5_proj_sample")
    nck = ss // L
    u_rows = u.reshape(ns, bs, nck, lw).transpose(0, 2, 1, 3).reshape(ns, 1, nck * bs, lw)
    st = jnp.concatenate([state_re.astype(F32).reshape(bs, ns, sg * p_),
                          state_im.astype(F32).reshape(bs, ns, sg * p_)], axis=-1)
    h0 = st.transpose(1, 0, 2)[None]
    y, hfin = _ssm(u_rows, h0, tables, nchains=bs, tile_rows=nck * bs, name="s5_ssm_sample")
    y = y.reshape(ns, nck, bs, lw).transpose(0, 2, 1, 3).reshape(ns, bs * nck, lw)
    xs_new = _s5_tail(y, u, g, d_vec, w_glu, w_out, xs, ln_g, ln_b, alpha, TM_OUT, "s5_tail_sample")
    re, im = split_state(hfin[0].transpose(1, 0, 2))
    new_s = (re.reshape(bs, groups, p_), im.reshape(bs, groups, p_))
    return xp_new, xs_new, new_p, new_s


def kernel(x_prompt, x_sample, cache_k_a, cache_v_a, cache_k_b, cache_v_b, state_re_c, state_im_c, ln_g, ln_b, w_in_a, w_out_a, w_in_b, sinks_b, w_out_b, w_in_c, a_re_c, a_im_c, log_dt_c, b_re_c, b_im_c, c_re_c, c_im_c, d_c, w_glu_c, w_out_c):
    depth = ln_g.shape[0]
    alpha = (2 * depth) ** 0.25
    bp, sp, d = x_prompt.shape
    bs, ss, _ = x_sample.shape
    xp = x_prompt.reshape(bp * sp, d)
    xs = x_sample.reshape(bs * ss, d)
    n_a = cache_k_a.shape[0]
    heads_a, hd_a = cache_k_a.shape[3], cache_k_a.shape[4]
    outs = {k: [] for k in ("ka_s", "va_s", "kb_p", "vb_p", "kb_s", "vb_s", "hr_p", "hi_p", "hr_s", "hi_s")}
    sb_stack = None
    for i in range(depth):
        j, kind = i // N_MIXERS, i % N_MIXERS
        if kind == 0:
            xp, xs, sb_stack, new_s = _sb_layer(xp, xs, bp, bs, cache_k_a, cache_v_a, w_in_a, w_out_a,
                                                ln_g[i], ln_b[i], alpha, (j, n_a, sb_stack))
            new_p, names = (), ("ka_s", "va_s")
        elif kind == 1:
            xp, xs, new_p, new_s = _swa_layer(xp, xs, bp, bs, cache_k_b[j], cache_v_b[j], w_in_b[j], sinks_b[j],
                                              w_out_b[j], ln_g[i], ln_b[i], alpha)
            names = ("kb_p", "vb_p", "kb_s", "vb_s")
        else:
            xp, xs, new_p, new_s = _s5_layer(xp, xs, bp, bs, state_re_c[j], state_im_c[j], w_in_c[j], a_re_c[j],
                                             a_im_c[j], log_dt_c[j], b_re_c[j], b_im_c[j], c_re_c[j], c_im_c[j],
                                             d_c[j], w_glu_c[j], w_out_c[j], ln_g[i], ln_b[i], alpha)
            names = ("hr_p", "hi_p", "hr_s", "hi_s")
        for nm, val in zip(names, tuple(new_p) + tuple(new_s)):
            outs[nm].append(val)
    ka_p = sb_stack[0].reshape(n_a, bp, sp, heads_a, hd_a)
    va_p = sb_stack[1].reshape(n_a, bp, sp, heads_a, hd_a)
    return (xp.reshape(bp, sp, d), xs.reshape(bs, ss, d),
            ka_p, va_p, jnp.stack(outs["ka_s"]), jnp.stack(outs["va_s"]),
            jnp.stack(outs["kb_p"]), jnp.stack(outs["vb_p"]), jnp.stack(outs["kb_s"]), jnp.stack(outs["vb_s"]),
            jnp.stack(outs["hr_p"]), jnp.stack(outs["hi_p"]), jnp.stack(outs["hr_s"]), jnp.stack(outs["hi_s"]))
```

```python
import functools
import math

import jax
import jax.numpy as jnp
from jax import lax
from jax.experimental import pallas as pl
from jax.experimental.pallas import tpu as pltpu

F32 = jnp.float32
BF16 = jnp.bfloat16

CHUNK = 64
WINDOW = 128
N_MIXERS = 3
SSM_GROUP = 16
LN_EPS = 1e-5
SB_KBLOCK = 128

LANES = 128
SUBLANES = 8
VMEM_LIMIT = 56 * 1024 * 1024

LOG2E = math.log2(math.e)
SB_DEAD_LOG2 = -127.0
SSM_L = SUBLANES
SSM_SLAB_GROUPS = LANES // SSM_GROUP


def _silu(x):
    return x * (1.0 / (1.0 + jnp.exp(-x)))


def _cparams(sem, vmem=VMEM_LIMIT):
    return pltpu.CompilerParams(dimension_semantics=sem, vmem_limit_bytes=vmem)


def _proj_kernel(*refs, segs, n_alias, step, stack_layer):
    x_ref, w_ref = refs[0], refs[1]
    out_refs = refs[2 + n_alias:]
    xb = x_ref[...].astype(BF16)
    outs = iter(out_refs)
    scr = out_refs[-1]
    for c0, width, scale, kinds in segs:
        o_refs = [next(outs) for _ in kinds]
        for s0 in range(0, width, step):
            s1 = min(width, s0 + step)
            acc = jnp.dot(xb, w_ref[:, c0 + s0:c0 + s1], preferred_element_type=F32)
            if scale != 1.0:
                acc = acc * scale
            for (kind, _), o_ref in zip(kinds, o_refs):
                if kind == "rows":
                    rows = acc.shape[0] // SUBLANES
                    for l0 in range(s0, s1, LANES):
                        slab = (l0 - s0) // LANES
                        scr[slab] = acc[:, l0 - s0:l0 - s0 + LANES]
                        for s in range(SUBLANES):
                            piece = scr[slab, pl.ds(s, rows, stride=SUBLANES), :]
                            o_ref[l0 // LANES, :, s * LANES:(s + 1) * LANES] = piece.astype(o_ref.dtype)
                elif kind == "stack_init":
                    for layer in range(o_ref.shape[0]):
                        val = acc.astype(o_ref.dtype) if layer == stack_layer else jnp.zeros(acc.shape, o_ref.dtype)
                        o_ref[layer, :, s0:s1] = val
                else:
                    o_ref[:, s0:s1] = acc.astype(o_ref.dtype)


def _resident_weight_spec(w, layer):
    if w.ndim == 3:
        return pl.BlockSpec((None,) + w.shape[1:], lambda i: (layer, 0, 0), pipeline_mode=pl.Buffered(1))
    return pl.BlockSpec(w.shape, lambda i: (0, 0), pipeline_mode=pl.Buffered(1))


def _proj(x2d, w_bf16, segs, tm, name, stack=None, w_layer=0):
    t, d = x2d.shape
    tm = min(tm, t)
    assert t % tm == 0 and tm % SUBLANES == 0
    step = 1024
    out_specs, out_shape, alias_in = [], [], []
    need_scratch = False
    for _, wd, _, kinds in segs:
        for kind, dt in kinds:
            if kind == "rows":
                need_scratch = True
                out_specs.append(pl.BlockSpec((wd // LANES, tm // SUBLANES, SUBLANES * LANES), lambda i: (0, i, 0)))
                out_shape.append(jax.ShapeDtypeStruct((wd // LANES, t // SUBLANES, SUBLANES * LANES), dt))
            elif kind == "stack_init":
                out_specs.append(pl.BlockSpec((stack[1], tm, wd), lambda i: (0, i, 0)))
                out_shape.append(jax.ShapeDtypeStruct((stack[1], t, wd), dt))
            elif kind == "stack":
                j, nl, prev = stack
                out_specs.append(pl.BlockSpec((None, tm, wd), functools.partial(lambda i, j_: (j_, i, 0), j_=j)))
                out_shape.append(jax.ShapeDtypeStruct((nl, t, wd), dt))
                alias_in.append((len(out_shape) - 1, prev[len(alias_in)]))
            else:
                out_specs.append(pl.BlockSpec((tm, wd), lambda i: (i, 0)))
                out_shape.append(jax.ShapeDtypeStruct((t, wd), dt))
    in_specs = [
        pl.BlockSpec((tm, d), lambda i: (i, 0)),
        _resident_weight_spec(w_bf16, w_layer),
    ] + [pl.BlockSpec(memory_space=pl.ANY) for _ in alias_in]
    scratch = [pltpu.VMEM((step // LANES, tm, LANES) if need_scratch else (1, SUBLANES, LANES), F32)]
    kern = functools.partial(_proj_kernel, segs=segs, n_alias=len(alias_in), step=step,
                             stack_layer=None if stack is None else stack[0])
    return pl.pallas_call(
        kern,
        grid=(t // tm,),
        in_specs=in_specs,
        out_specs=out_specs,
        out_shape=out_shape,
        scratch_shapes=scratch,
        input_output_aliases={2 + k: oi for k, (oi, _) in enumerate(alias_in)},
        compiler_params=_cparams(("parallel",)),
        name=name,
    )(x2d, w_bf16, *[buf for _, buf in alias_in])


def _layer_norm_rows(r, g, b):
    mu = jnp.mean(r, axis=-1, keepdims=True)
    rc = r - mu
    var = jnp.mean(rc * rc, axis=-1, keepdims=True)
    return rc * lax.rsqrt(var + LN_EPS) * g + b


def _outproj_kernel(a_ref, w_ref, x_ref, g_ref, b_ref, o_ref, *, alpha, sub):
    for r0 in range(0, a_ref.shape[0], sub):
        y = jnp.dot(a_ref[r0:r0 + sub, :], w_ref[...], preferred_element_type=F32)
        r = alpha * x_ref[r0:r0 + sub, :] + y
        o_ref[r0:r0 + sub, :] = _layer_norm_rows(r, g_ref[...], b_ref[...])


def _outproj_ln(a2d, w_bf16, x2d, ln_g, ln_b, alpha, tm, name, w_layer=0):
    t, wdt = a2d.shape
    d = x2d.shape[1]
    tm = min(tm, t)
    assert t % tm == 0
    return pl.pallas_call(
        functools.partial(_outproj_kernel, alpha=alpha, sub=min(tm, OUT_SUB_ROWS)),
        grid=(t // tm,),
        in_specs=[
            pl.BlockSpec((tm, wdt), lambda i: (i, 0)),
            _resident_weight_spec(w_bf16, w_layer),
            pl.BlockSpec((tm, d), lambda i: (i, 0)),
            pl.BlockSpec((1, d), lambda i: (0, 0)),
            pl.BlockSpec((1, d), lambda i: (0, 0)),
        ],
        out_specs=pl.BlockSpec((tm, d), lambda i: (i, 0)),
        out_shape=jax.ShapeDtypeStruct((t, d), F32),
        compiler_params=_cparams(("parallel",)),
        name=name,
    )(a2d, w_bf16, x2d, ln_g.reshape(1, d).astype(F32), ln_b.reshape(1, d).astype(F32))


def _scores(qs, kblks):
    return [lax.dot_general(q, k, (((1,), (1,)), ((), ())), preferred_element_type=F32) for q, k in zip(qs, kblks)]


def _sb_blocks(zs, vblks, masks, laters, accs, tri):
    n = len(zs)
    rows = zs[0].shape[0]
    lbetas, cats = [], []
    for c in range(n):
        z = zs[c]
        sp = jnp.maximum(z, 0.0) + jnp.log2(1.0 + jnp.exp2(-jnp.abs(z)))
        lbetas.append(z - sp)
        if masks[c] is not None:
            sp = jnp.where(masks[c], sp, 0.0)
        hi = sp.astype(BF16)
        lo = (sp - hi.astype(F32)).astype(BF16)
        cats.append(jnp.concatenate([hi, lo], axis=1))
    cs = jnp.dot(jnp.concatenate(cats, axis=0), tri, preferred_element_type=F32)
    new_l, ws = [], []
    for c in range(n):
        blk = cs[c * rows:(c + 1) * rows]
        w = jnp.exp2(lbetas[c] + blk[:, :SB_KBLOCK] + laters[c])
        if masks[c] is not None:
            w = jnp.where(masks[c], w, 0.0)
        ws.append(w.astype(BF16))
        new_l.append(laters[c] + blk[:, SB_KBLOCK:])
    new_a = [accs[c] + jnp.dot(ws[c], vblks[c], preferred_element_type=F32) for c in range(n)]
    return new_l, new_a


def _sb_alive(ls):
    m = ls[0]
    for l in ls[1:]:
        m = jnp.maximum(m, l)
    return jnp.max(m) > SB_DEAD_LOG2


def _sb_tri():
    j = jnp.arange(SB_KBLOCK)
    later = -(j[:, None] > j[None, :]).astype(BF16)
    ones = -jnp.ones((SB_KBLOCK, SB_KBLOCK), BF16)
    one = jnp.concatenate([later, ones], axis=1)
    return jnp.concatenate([one, one], axis=0)


def _sb_prompt_kernel(q_ref, k_ref, v_ref, g_ref, tri_ref, o_ref, *, nchain):
    kb = SB_KBLOCK
    i = pl.program_id(2)
    n0 = [i * nchain + c for c in range(nchain)]
    n0_max = n0[-1]
    col = lax.broadcasted_iota(jnp.int32, (kb, kb), 1)
    row = lax.broadcasted_iota(jnp.int32, (kb, kb), 0)

    def offs(nbs):
        return [pl.multiple_of(jnp.maximum(nb, 0) * kb, kb) for nb in nbs]

    def loadk(nbs):
        return [k_ref[0, pl.ds(o, kb), :] for o in offs(nbs)]

    def loadv(nbs):
        return [v_ref[0, pl.ds(o, kb), :] for o in offs(nbs)]

    def queries():
        return [q_ref[0, c * kb:(c + 1) * kb, :] for c in range(nchain)]

    zero = jnp.zeros((kb, kb), F32)
    diag = col < row
    laters, accs = _sb_blocks(_scores(queries(), loadk(n0)), loadv(n0), [diag] * nchain,
                              [zero] * nchain, [zero] * nchain, tri_ref[...])

    def cond(st):
        return jnp.logical_and(st[0] <= n0_max, st[1])

    def earlier_block(t, laters, accs):
        nbs = [n0[c] - t for c in range(nchain)]
        ls = [laters[c] + jnp.where(nbs[c] >= 0, 0.0, -1e30).astype(F32) for c in range(nchain)]
        return _sb_blocks(_scores(queries(), loadk(nbs)), loadv(nbs), [None] * nchain, ls, list(accs), tri_ref[...])

    laters, accs = earlier_block(1, laters, accs)

    def body(st):
        ls, as_ = earlier_block(st[0], st[2], st[3])
        return (st[0] + 1, _sb_alive(ls), tuple(ls), tuple(as_))

    st = lax.while_loop(cond, body, (jnp.int32(2), _sb_alive(laters), tuple(laters), tuple(accs)))
    accs = st[3]
    for c in range(nchain):
        g = g_ref[0, c * kb:(c + 1) * kb, :].astype(F32)
        o_ref[0, c * kb:(c + 1) * kb, :] = (accs[c] * _silu(g)).astype(o_ref.dtype)


def _sb_attention_prompt(q, k, v, g, heads, nchain, name):
    b, s, w = q.shape
    hd = w // heads
    tq = SB_KBLOCK * nchain
    assert hd == SB_KBLOCK and s % tq == 0
    qmap = lambda bi, h, i: (bi, i, h)
    kmap = lambda bi, h, i: (bi, 0, h)
    return pl.pallas_call(
        functools.partial(_sb_prompt_kernel, nchain=nchain),
        grid=(b, heads, s // tq),
        in_specs=[
            pl.BlockSpec((1, tq, hd), qmap),
            pl.BlockSpec((1, s, hd), kmap),
            pl.BlockSpec((1, s, hd), kmap),
            pl.BlockSpec((1, tq, hd), qmap),
            pl.BlockSpec((2 * SB_KBLOCK, 2 * SB_KBLOCK), lambda bi, h, i: (0, 0)),
        ],
        out_specs=pl.BlockSpec((1, tq, hd), qmap),
        out_shape=jax.ShapeDtypeStruct((b, s, w), BF16),
        compiler_params=_cparams(("parallel", "parallel", "arbitrary")),
        name=name,
    )(q, k, v, g, _sb_tri())


def _sb_sample_kernel(q_ref, kn_ref, vn_ref, g_ref, ck_ref, cv_ref, ckh_ref, cvh_ref, tri_ref, o_ref,
                      kscr, vscr, sem, *, heads, wblk, nrem, layer):
    kb = SB_KBLOCK
    b = pl.program_id(0)
    ss = q_ref.shape[1]
    qs = [q_ref[0, :, h * kb:(h + 1) * kb] for h in range(heads)]
    zpad = jnp.zeros((kb - ss, kb), BF16)
    col = lax.broadcasted_iota(jnp.int32, (ss, kb), 1)
    row = lax.broadcasted_iota(jnp.int32, (ss, kb), 0)
    new_mask = col < row
    kbl = [jnp.concatenate([kn_ref[0, :, h * kb:(h + 1) * kb], zpad], axis=0) for h in range(heads)]
    vbl = [jnp.concatenate([vn_ref[0, :, h * kb:(h + 1) * kb], zpad], axis=0) for h in range(heads)]
    zero = jnp.zeros((ss, kb), F32)
    laters, accs = _sb_blocks(_scores(qs, kbl), vbl, [new_mask] * heads, [zero] * heads, [zero] * heads, tri_ref[...])

    def head_blocks(ref, r0):
        return [ref[pl.ds(r0 + h, kb, stride=heads), :].astype(BF16) for h in range(heads)]

    for j in range(wblk):
        r0 = (wblk - 1 - j) * kb * heads
        laters, accs = _sb_blocks(_scores(qs, head_blocks(ck_ref.at[0], r0)), head_blocks(cv_ref.at[0], r0),
                                  [None] * heads, laters, accs, tri_ref[...])

    if nrem > 0:
        def cond(st):
            return jnp.logical_and(st[0] < nrem, st[1])

        def body(st):
            t = st[0]
            r0 = pl.multiple_of((nrem - 1 - t) * (kb * heads), kb * heads)
            ck = pltpu.make_async_copy(ckh_ref.at[layer, b, pl.ds(r0, kb * heads)], kscr, sem.at[0])
            cv = pltpu.make_async_copy(cvh_ref.at[layer, b, pl.ds(r0, kb * heads)], vscr, sem.at[1])
            ck.start()
            cv.start()
            ck.wait()
            cv.wait()
            ls, as_ = _sb_blocks(_scores(qs, head_blocks(kscr, 0)), head_blocks(vscr, 0), [None] * heads,
                                 list(st[2]), list(st[3]), tri_ref[...])
            return (t + 1, _sb_alive(ls), tuple(ls), tuple(as_))

        st = lax.while_loop(cond, body, (jnp.int32(0), _sb_alive(laters), tuple(laters), tuple(accs)))
        accs = st[3]
    for h in range(heads):
        g = g_ref[0, :, h * kb:(h + 1) * kb].astype(F32)
        o_ref[0, :, h * kb:(h + 1) * kb] = (accs[h] * _silu(g)).astype(o_ref.dtype)


def _sb_attention_sample(q, k_new, v_new, g, cache_k, cache_v, layer, wblk, name):
    b, ss, w = q.shape
    nl, _, past, heads, hd = cache_k.shape
    assert hd == SB_KBLOCK and w == heads * hd and ss <= SB_KBLOCK and ss % 16 == 0
    wblk = min(wblk, past // SB_KBLOCK)
    tail = wblk * SB_KBLOCK
    assert past % tail == 0
    nrem = past // SB_KBLOCK - wblk
    ck = cache_k.reshape(nl, b, past * heads, hd)
    cv = cache_v.reshape(nl, b, past * heads, hd)
    tok = lambda bi: (bi, 0, 0)
    tail_map = lambda bi: (layer, bi, past // tail - 1, 0)
    kern = functools.partial(_sb_sample_kernel, heads=heads, wblk=wblk, nrem=nrem, layer=layer)
    return pl.pallas_call(
        kern,
        grid=(b,),
        in_specs=[
            pl.BlockSpec((1, ss, w), tok),
            pl.BlockSpec((1, ss, w), tok),
            pl.BlockSpec((1, ss, w), tok),
            pl.BlockSpec((1, ss, w), tok),
            pl.BlockSpec((None, 1, tail * heads, hd), tail_map),
            pl.BlockSpec((None, 1, tail * heads, hd), tail_map),
            pl.BlockSpec(memory_space=pl.ANY),
            pl.BlockSpec(memory_space=pl.ANY),
            pl.BlockSpec((2 * SB_KBLOCK, 2 * SB_KBLOCK), lambda bi: (0, 0)),
        ],
        out_specs=pl.BlockSpec((1, ss, w), tok),
        out_shape=jax.ShapeDtypeStruct((b, ss, w), BF16),
        scratch_shapes=[pltpu.VMEM((SB_KBLOCK * heads, hd), F32), pltpu.VMEM((SB_KBLOCK * heads, hd), F32),
                        pltpu.SemaphoreType.DMA((2,))],
        compiler_params=_cparams(("arbitrary",)),
        name=name,
    )(q, k_new, v_new, g, ck, cv, ck, cv, _sb_tri())


def _swa_kernel(q_ref, kp_ref, kc_ref, vp_ref, vc_ref, g_ref, bias_ref, sink_ref, o_ref,
                *, cq, nch, span, kvh, rep, hd, mask_first):
    i = pl.program_id(1)
    kbuf = jnp.concatenate([kp_ref[0], kc_ref[0]], axis=0)
    vbuf = jnp.concatenate([vp_ref[0], vc_ref[0]], axis=0)
    nprev = kp_ref.shape[1]
    gw = rep * hd
    for kv in range(kvh):
        kk = kbuf[:, kv * hd:(kv + 1) * hd]
        vv = vbuf[:, kv * hd:(kv + 1) * hd]
        bias = bias_ref[kv]
        sink = sink_ref[kv]
        sts = []
        for j in range(nch):
            ks = kk[j * cq:j * cq + span]
            qs = q_ref[0, j * cq:(j + 1) * cq, kv * gw:(kv + 1) * gw]
            lhs = jnp.concatenate([qs[:, r * hd:(r + 1) * hd] for r in range(rep)], axis=0)
            st = lax.dot_general(ks, lhs, (((1,), (1,)), ((), ())), preferred_element_type=F32) + bias
            if mask_first and j * cq < nprev:
                buf_row = j * cq + lax.broadcasted_iota(jnp.int32, st.shape, 0)
                valid = jnp.logical_or(i > 0, buf_row >= nprev)
                st = jnp.where(valid, st, -jnp.inf)
            sts.append(st)
        ps = []
        for st in sts:
            m = jnp.maximum(jnp.max(st, axis=0, keepdims=True), sink)
            p = jnp.exp2(st - m)
            den = jnp.sum(p, axis=0, keepdims=True) + jnp.exp2(sink - m)
            ps.append((p * (1.0 / den)).astype(BF16))
        for j in range(nch):
            vs = vv[j * cq:j * cq + span]
            o = lax.dot_general(ps[j], vs, (((0,), (0,)), ((), ())), preferred_element_type=F32)
            ot = jnp.concatenate([o[r * cq:(r + 1) * cq] for r in range(rep)], axis=1)
            g = g_ref[0, j * cq:(j + 1) * cq, kv * gw:(kv + 1) * gw].astype(F32)
            o_ref[0, j * cq:(j + 1) * cq, kv * gw:(kv + 1) * gw] = (ot * _silu(g)).astype(o_ref.dtype)


def _alibi_bias_t(kvh, rep, cq, span, nprev):
    n = kvh * rep
    slopes = 2.0 ** (-8.0 * jnp.arange(1, n + 1, dtype=F32) / n)
    qi = jnp.arange(cq, dtype=jnp.int32)
    kj = jnp.arange(span, dtype=jnp.int32)
    dist = jnp.abs(qi[None, :] + nprev - kj[:, None]).astype(F32)
    bias = -(slopes * LOG2E).reshape(kvh, 1, rep, 1) * dist[None, :, None, :]
    return bias.reshape(kvh, span, rep * cq)


def _swa_attention(q, k_prev_src, k_cur, v_prev_src, v_cur, g, sinks, *, cq, nch, nprev, prev_from_cur, name):
    b, sq, qw = q.shape
    kvw = k_cur.shape[2]
    kvh, rep = sinks.shape
    hd = kvw // kvh
    rows = cq * nch
    span = nprev + cq
    assert sq % rows == 0 and qw == kvh * rep * hd
    if prev_from_cur:
        assert rows % nprev == 0
        ratio = rows // nprev
        prev_map = lambda bi, i: (bi, jnp.maximum(i * ratio - 1, 0), 0)
    else:
        assert sq == rows
        prev_map = lambda bi, i: (bi, 0, 0)
    cur_map = lambda bi, i: (bi, i, 0)
    const3 = lambda bi, i: (0, 0, 0)
    sink_t = jnp.broadcast_to((sinks.astype(F32) * LOG2E)[:, None, :, None], (kvh, 1, rep, cq)).reshape(kvh, 1, rep * cq)
    kern = functools.partial(_swa_kernel, cq=cq, nch=nch, span=span, kvh=kvh, rep=rep, hd=hd,
                             mask_first=prev_from_cur)
    return pl.pallas_call(
        kern,
        grid=(b, sq // rows),
        in_specs=[
            pl.BlockSpec((1, rows, qw), cur_map),
            pl.BlockSpec((1, nprev, kvw), prev_map),
            pl.BlockSpec((1, rows, kvw), cur_map),
            pl.BlockSpec((1, nprev, kvw), prev_map),
            pl.BlockSpec((1, rows, kvw), cur_map),
            pl.BlockSpec((1, rows, qw), cur_map),
            pl.BlockSpec((kvh, span, rep * cq), const3),
            pl.BlockSpec((kvh, 1, rep * cq), const3),
        ],
        out_specs=pl.BlockSpec((1, rows, qw), cur_map),
        out_shape=jax.ShapeDtypeStruct((b, sq, qw), BF16),
        compiler_params=_cparams(("parallel", "arbitrary")),
        name=name,
    )(q, k_prev_src, k_cur, v_prev_src, v_cur, g, _alibi_bias_t(kvh, rep, cq, span, nprev), sink_t)


def _ssm_kernel(u_ref, w_ref, m_ref, v_ref, lam_ref, h0_ref, y_ref, hfin_ref, s_scr, h_scr, carry, *, nchains):
    ti = pl.program_id(2)
    nrow = s_scr.shape[0]
    half = s_scr.shape[1] // 2
    steps = nrow // nchains

    @pl.when(ti == 0)
    def _():
        carry[...] = h0_ref[0, 0]

    u = u_ref[0, 0]
    s_scr[...] = jnp.dot(u, w_ref[0], preferred_element_type=F32)

    lam = lam_ref[0]
    lr = jnp.broadcast_to(lam[:, :half], (nchains, half))
    li = jnp.broadcast_to(lam[:, half:], (nchains, half))

    def step(c, h):
        r0 = pl.multiple_of(c * nchains, nchains)
        h_scr[pl.ds(r0, nchains), :] = h
        s = s_scr[pl.ds(r0, nchains), :]
        hr, hi = h[:, :half], h[:, half:]
        nr = lr * hr - li * hi + s[:, :half]
        ni = lr * hi + li * hr + s[:, half:]
        return jnp.concatenate([nr, ni], axis=1)

    h = lax.fori_loop(0, steps, step, carry[...], unroll=(8 if steps % 8 == 0 else 1))
    carry[...] = h
    hfin_ref[0, 0] = h

    y_ref[0, 0] = (jnp.dot(u, m_ref[0], preferred_element_type=F32)
                   + jnp.dot(h_scr[...].astype(BF16), v_ref[0], preferred_element_type=F32))


def _ssm_dense_kernel(u_ref, w_ref, m_ref, v_ref, lam_ref, h0_ref, y_ref, hfin_ref,
                      sre, sim, hre, him, cre, cim, *, nb, kt):
    ti = pl.program_id(2)
    rows = u_ref.shape[2]
    half = kt * LANES

    def fold(row_of):
        return jnp.concatenate([row_of(b, k) for b in range(nb) for k in range(kt)], axis=0)

    @pl.when(ti == 0)
    def _():
        cre[...] = fold(lambda b, k: h0_ref[b, 0, :, k * LANES:(k + 1) * LANES])
        cim[...] = fold(lambda b, k: h0_ref[b, 0, :, half + k * LANES:half + (k + 1) * LANES])

    us = [u_ref[0, b] for b in range(nb)]
    for b in range(nb):
        s = jnp.dot(us[b], w_ref[0], preferred_element_type=F32)
        for k in range(kt):
            sre[pl.ds(b * kt + k, rows, stride=SUBLANES), :] = s[:, k * LANES:(k + 1) * LANES]
            sim[pl.ds(b * kt + k, rows, stride=SUBLANES), :] = s[:, half + k * LANES:half + (k + 1) * LANES]

    lam = lam_ref[0]
    lr = fold(lambda b, k: lam[:, k * LANES:(k + 1) * LANES])
    li = fold(lambda b, k: lam[:, half + k * LANES:half + (k + 1) * LANES])

    def step(c, carry):
        hr, hi = carry
        r0 = pl.multiple_of(c * SUBLANES, SUBLANES)
        hre[pl.ds(r0, SUBLANES), :] = hr
        him[pl.ds(r0, SUBLANES), :] = hi
        nr = lr * hr - li * hi + sre[pl.ds(r0, SUBLANES), :]
        ni = lr * hi + li * hr + sim[pl.ds(r0, SUBLANES), :]
        return nr, ni

    hr, hi = lax.fori_loop(0, rows, step, (cre[...], cim[...]), unroll=(8 if rows % 8 == 0 else 1))
    cre[...] = hr
    cim[...] = hi
    for b in range(nb):
        pieces = ([hr[b * kt + k:b * kt + k + 1] for k in range(kt)]
                  + [hi[b * kt + k:b * kt + k + 1] for k in range(kt)])
        hfin_ref[b, 0] = jnp.concatenate(pieces, axis=1)
        hcat = jnp.concatenate(
            [hre[pl.ds(b * kt + k, rows, stride=SUBLANES), :] for k in range(kt)]
            + [him[pl.ds(b * kt + k, rows, stride=SUBLANES), :] for k in range(kt)], axis=1)
        yv = jnp.dot(hcat.astype(BF16), v_ref[0], preferred_element_type=F32)
        cw = 2 * LANES
        for c0 in range(0, m_ref.shape[2], cw):
            y_ref[0, b, :, c0:c0 + cw] = yv[:, c0:c0 + cw] + jnp.dot(
                us[b][:, :c0 + cw], m_ref[0, :c0 + cw, c0:c0 + cw], preferred_element_type=F32)


def _ssm_dense(u_rows, h0, tables, *, tile_rows, name):
    w, m, v, lam = tables
    ns, b, r, lw = u_rows.shape
    sw = w.shape[2]
    kt = sw // (2 * LANES)
    nb = SUBLANES // kt
    tile_rows = min(tile_rows, r)
    assert r % tile_rows == 0 and b % nb == 0 and nb * kt == SUBLANES
    row_map = lambda o, bi, ti: (o, bi, ti, 0)
    tab = lambda o, bi, ti: (o, 0, 0)
    st_map = lambda o, bi, ti: (bi, o, 0, 0)
    fold_scr = pltpu.VMEM((tile_rows * SUBLANES, LANES), F32)
    return pl.pallas_call(
        functools.partial(_ssm_dense_kernel, nb=nb, kt=kt),
        grid=(ns, b // nb, r // tile_rows),
        in_specs=[
            pl.BlockSpec((1, nb, tile_rows, lw), row_map),
            pl.BlockSpec((1, lw, sw), tab),
            pl.BlockSpec((1, lw, lw), tab),
            pl.BlockSpec((1, sw, lw), tab),
            pl.BlockSpec((1, 1, sw), tab),
            pl.BlockSpec((nb, 1, 1, sw), st_map),
        ],
        out_specs=[
            pl.BlockSpec((1, nb, tile_rows, lw), row_map),
            pl.BlockSpec((nb, 1, 1, sw), st_map),
        ],
        out_shape=[jax.ShapeDtypeStruct((ns, b, r, lw), F32),
                   jax.ShapeDtypeStruct((b, ns, 1, sw), F32)],
        scratch_shapes=[fold_scr, fold_scr, fold_scr, fold_scr,
                        pltpu.VMEM((SUBLANES, LANES), F32), pltpu.VMEM((SUBLANES, LANES), F32)],
        compiler_params=_cparams(("parallel", "parallel", "arbitrary")),
        name=name,
    )(u_rows, w, m, v, lam, h0)


def _ssm_tables(a_re, a_im, log_dt, b_re, b_im, c_re, c_im, L):
    g_, p_ = a_re.shape
    sg = SSM_SLAB_GROUPS
    ns = g_ // sg
    hp = lax.Precision.HIGHEST
    a_re = jnp.minimum(a_re.astype(F32), -1e-4)
    a_im = a_im.astype(F32)
    dt = jnp.exp(log_dt.astype(F32))[:, None]
    mag = jnp.exp(dt * a_re)
    lb_re = mag * jnp.cos(dt * a_im)
    lb_im = mag * jnp.sin(dt * a_im)
    n_re = lb_re - 1.0
    den = a_re * a_re + a_im * a_im
    f_re = ((n_re * a_re + lb_im * a_im) / den)[..., None]
    f_im = ((lb_im * a_re - n_re * a_im) / den)[..., None]
    br, bi = b_re.astype(F32), b_im.astype(F32)
    bb_re = f_re * br - f_im * bi
    bb_im = f_re * bi + f_im * br
    cr, ci = c_re.astype(F32), c_im.astype(F32)
    tau = jnp.arange(L + 1, dtype=F32)[:, None, None]
    pmag = jnp.exp(tau * dt * a_re)
    pw_re = pmag * jnp.cos(tau * dt * a_im)
    pw_im = pmag * jnp.sin(tau * dt * a_im)

    lb_r = pw_re[:, :, :, None] * bb_re[None] - pw_im[:, :, :, None] * bb_im[None]
    lb_i = pw_re[:, :, :, None] * bb_im[None] + pw_im[:, :, :, None] * bb_re[None]
    kt = (jnp.einsum('gop,tgpi->tgoi', cr, lb_r[:L], precision=hp)
          - jnp.einsum('gop,tgpi->tgoi', ci, lb_i[:L], precision=hp))
    lag = jnp.arange(L)[None, :] - jnp.arange(L)[:, None]
    kst = jnp.where((lag >= 0)[:, :, None, None, None], kt[jnp.clip(lag, 0, L - 1)], 0.0)

    rows_u = L * LANES
    cols_s = 2 * sg * p_
    rid = jnp.arange(rows_u)
    grp_u = (rid % LANES) // SSM_GROUP
    sid = jnp.arange(cols_s)
    grp_s = (sid % (sg * p_)) // p_

    def expand(compact, rep_mat, grp_r, grp_c):
        full = jnp.einsum('nrc,ck->nrk', compact.astype(BF16), rep_mat.astype(BF16),
                          preferred_element_type=BF16)
        return jnp.where(grp_r[:, None] == grp_c[None, :], full, jnp.zeros((), BF16))

    m_c = kst.reshape(L, L, ns, sg, SSM_GROUP, SSM_GROUP).transpose(2, 0, 3, 5, 1, 4).reshape(ns, rows_u, L * SSM_GROUP)
    cid = jnp.arange(rows_u)
    rep_to = ((cid[None, :] // LANES == jnp.arange(L * SSM_GROUP)[:, None] // SSM_GROUP)
              & (cid[None, :] % SSM_GROUP == jnp.arange(L * SSM_GROUP)[:, None] % SSM_GROUP)).astype(F32)
    m = expand(m_c, rep_to, grp_u, grp_u)
    wri = jnp.stack([lb_r[:L][::-1], lb_i[:L][::-1]], axis=0).reshape(2, L, ns, sg, p_, SSM_GROUP)
    w_c = wri.transpose(2, 1, 3, 5, 0, 4).reshape(ns, rows_u, 2 * p_)
    rep_ap = ((sid[None, :] // (sg * p_) == jnp.arange(2 * p_)[:, None] // p_)
              & (sid[None, :] % p_ == jnp.arange(2 * p_)[:, None] % p_)).astype(F32)
    w = expand(w_c, rep_ap, grp_u, grp_s)
    cl_r = cr[None] * pw_re[1:][:, :, None, :] - ci[None] * pw_im[1:][:, :, None, :]
    cl_i = cr[None] * pw_im[1:][:, :, None, :] + ci[None] * pw_re[1:][:, :, None, :]
    vri = jnp.stack([cl_r, -cl_i], axis=0).reshape(2, L, ns, sg, SSM_GROUP, p_)
    v_c = vri.transpose(2, 0, 3, 5, 1, 4).reshape(ns, cols_s, L * SSM_GROUP)
    v = expand(v_c, rep_to, grp_s, grp_u)
    lam = jnp.stack([pw_re[L], pw_im[L]], axis=0).reshape(2, ns, sg * p_)
    lam = jnp.transpose(lam, (1, 0, 2)).reshape(ns, 1, cols_s)
    return w, m, v, lam


def _ssm(u_rows, h0, tables, *, nchains, tile_rows, name):
    w, m, v, lam = tables
    ns, b, r, lw = u_rows.shape
    sw = w.shape[2]
    tile_rows = min(tile_rows, r)
    assert r % tile_rows == 0 and tile_rows % nchains == 0
    row_map = lambda o, bi, ti: (o, bi, ti, 0)
    tab = lambda o, bi, ti: (o, 0, 0)
    st_map = lambda o, bi, ti: (bi, o, 0, 0)
    return pl.pallas_call(
        functools.partial(_ssm_kernel, nchains=nchains),
        grid=(ns, b, r // tile_rows),
        in_specs=[
            pl.BlockSpec((1, 1, tile_rows, lw), row_map),
            pl.BlockSpec((1, lw, sw), tab),
            pl.BlockSpec((1, lw, lw), tab),
            pl.BlockSpec((1, sw, lw), tab),
            pl.BlockSpec((1, 1, sw), tab),
            pl.BlockSpec((1, 1, nchains, sw), st_map),
        ],
        out_specs=[
            pl.BlockSpec((1, 1, tile_rows, lw), row_map),
            pl.BlockSpec((1, 1, nchains, sw), st_map),
        ],
        out_shape=[jax.ShapeDtypeStruct((ns, b, r, lw), F32),
                   jax.ShapeDtypeStruct((b, ns, nchains, sw), F32)],
        scratch_shapes=[pltpu.VMEM((tile_rows, sw), F32), pltpu.VMEM((tile_rows, sw), F32),
                        pltpu.VMEM((nchains, sw), F32)],
        compiler_params=_cparams(("parallel", "parallel", "arbitrary")),
        name=name,
    )(u_rows, w, m, v, lam, h0)


def _gelu_tanh(x):
    return 0.5 * x * (1.0 + jnp.tanh(math.sqrt(2.0 / math.pi) * (x + 0.044715 * (x * x * x))))


def _s5_tail_kernel(y_ref, u_ref, g_ref, d_ref, wg_ref, wo_ref, x_ref, lg_ref, lb_ref, o_ref, y_scr, u_scr, *, alpha):
    ns, rows, _ = y_ref.shape
    for o in range(ns):
        for s in range(SUBLANES):
            y_scr[o, pl.ds(s, rows, stride=SUBLANES), :] = y_ref[o, :, s * LANES:(s + 1) * LANES]
            u_scr[o, pl.ds(s, rows, stride=SUBLANES), :] = u_ref[o, :, s * LANES:(s + 1) * LANES].astype(F32)
    y_tok = jnp.concatenate([y_scr[o] for o in range(ns)], axis=1)
    u_tok = jnp.concatenate([u_scr[o] for o in range(ns)], axis=1)
    y = _gelu_tanh(y_tok + d_ref[...] * u_tok)
    z = jnp.dot(y.astype(BF16), wg_ref[...], preferred_element_type=F32)
    y = y * (1.0 / (1.0 + jnp.exp(-z)))
    a = (y * _silu(g_ref[...].astype(F32))).astype(BF16)
    out = jnp.dot(a, wo_ref[...], preferred_element_type=F32)
    r = alpha * x_ref[...] + out
    o_ref[...] = _layer_norm_rows(r, lg_ref[...], lb_ref[...])


def _s5_tail(y_rows, u_rows, g2d, d_vec, w_glu, w_out, x2d, ln_g, ln_b, alpha, tm, name):
    ns, tr, lw = y_rows.shape
    t = tr * SUBLANES
    c = ns * LANES
    d = x2d.shape[1]
    tm = min(tm, t)
    assert t % tm == 0 and tm % SUBLANES == 0 and lw == SUBLANES * LANES
    row_spec = pl.BlockSpec((ns, tm // SUBLANES, lw), lambda i: (0, i, 0))
    const = lambda i: (0, 0)
    return pl.pallas_call(
        functools.partial(_s5_tail_kernel, alpha=alpha),
        grid=(t // tm,),
        in_specs=[
            row_spec, row_spec,
            pl.BlockSpec((tm, c), lambda i: (i, 0)),
            pl.BlockSpec((1, c), const),
            pl.BlockSpec((c, c), const, pipeline_mode=pl.Buffered(1)),
            pl.BlockSpec((c, d), const, pipeline_mode=pl.Buffered(1)),
            pl.BlockSpec((tm, d), lambda i: (i, 0)),
            pl.BlockSpec((1, d), const),
            pl.BlockSpec((1, d), const),
        ],
        out_specs=pl.BlockSpec((tm, d), lambda i: (i, 0)),
        out_shape=jax.ShapeDtypeStruct((t, d), F32),
        scratch_shapes=[pltpu.VMEM((ns, tm, LANES), F32), pltpu.VMEM((ns, tm, LANES), F32)],
        compiler_params=_cparams(("parallel",)),
        name=name,
    )(y_rows, u_rows, g2d, d_vec.reshape(1, c).astype(F32), w_glu, w_out, x2d,
      ln_g.reshape(1, d).astype(F32), ln_b.reshape(1, d).astype(F32))


TM_PROJ = 512
TM_OUT = 512
TM_OUT_SB = 1024
OUT_SUB_ROWS = 256
SB_CHAINS = 16
SB_SAMPLE_TAIL_BLOCKS = 2
SWA_CHUNKS_PER_STEP = 4
SSM_TILE_ROWS = 512


def _sb_layer(xp, xs, bp, bs, cache_k, cache_v, w_in, w_out, ln_g, ln_b, alpha, stack_p):
    layer = stack_p[0]
    heads, hd = cache_k.shape[3], cache_k.shape[4]
    width = heads * hd
    sp, ss = xp.shape[0] // bp, xs.shape[0] // bs
    w_in = w_in.astype(BF16)
    w_out = w_out.astype(BF16)

    def segs(kv_kind):
        return ((3 * width, width, 1.0, (("plain", BF16),)),
                (0, width, LOG2E * hd ** -0.5, (("plain", BF16),)),
                (width, width, 1.0, (("plain", BF16), (kv_kind, F32))),
                (2 * width, width, 1.0, (("plain", BF16), (kv_kind, F32))))

    r3 = lambda a, b_, s_: a.reshape(b_, s_, width)
    kv_kind = "stack_init" if stack_p[2] is None else "stack"
    g, q, kb, kf, vb, vf = _proj(xp, w_in, segs(kv_kind), TM_PROJ, "sb_proj_prompt", stack=stack_p, w_layer=layer)
    og = _sb_attention_prompt(r3(q, bp, sp), r3(kb, bp, sp), r3(vb, bp, sp), r3(g, bp, sp), heads,
                              nchain=min(SB_CHAINS, sp // SB_KBLOCK), name="sb_attn_prompt")
    xp_new = _outproj_ln(og.reshape(bp * sp, width), w_out, xp, ln_g, ln_b, alpha, TM_OUT_SB, "sb_out_prompt",
                         w_layer=layer)
    g, q, kb, kf_s, vb, vf_s = _proj(xs, w_in, segs("plain"), TM_PROJ, "sb_proj_sample", w_layer=layer)
    og = _sb_attention_sample(r3(q, bs, ss), r3(kb, bs, ss), r3(vb, bs, ss), r3(g, bs, ss), cache_k, cache_v,
                              layer, SB_SAMPLE_TAIL_BLOCKS, "sb_attn_sample")
    xs_new = _outproj_ln(og.reshape(bs * ss, width), w_out, xs, ln_g, ln_b, alpha, TM_OUT_SB, "sb_out_sample",
                         w_layer=layer)
    new_s = (kf_s.reshape(bs, ss, heads, hd), vf_s.reshape(bs, ss, heads, hd))
    return xp_new, xs_new, (kf, vf), new_s


def _swa_layer(xp, xs, bp, bs, cache_k, cache_v, w_in, sinks, w_out, ln_g, ln_b, alpha):
    kvh, hd = cache_k.shape[2], cache_k.shape[3]
    rep = sinks.shape[1]
    qw, kvw = kvh * rep * hd, kvh * hd
    nprev = cache_k.shape[1]
    sp, ss = xp.shape[0] // bp, xs.shape[0] // bs
    assert nprev == WINDOW and sp >= WINDOW
    w_in = w_in.astype(BF16)
    w_out = w_out.astype(BF16)
    segs = ((qw + 2 * kvw, qw, 1.0, (("plain", BF16),)),
            (0, qw, LOG2E * hd ** -0.5, (("plain", BF16),)),
            (qw, kvw, 1.0, (("plain", BF16), ("plain", F32))),
            (qw + kvw, kvw, 1.0, (("plain", BF16), ("plain", F32))))
    g, q, kb, kf, vb, vf = _proj(xp, w_in, segs, TM_PROJ, "swa_proj_prompt")
    k3, v3 = kb.reshape(bp, sp, kvw), vb.reshape(bp, sp, kvw)
    og = _swa_attention(q.reshape(bp, sp, qw), k3, k3, v3, v3, g.reshape(bp, sp, qw), sinks,
                        cq=CHUNK, nch=SWA_CHUNKS_PER_STEP, nprev=WINDOW, prev_from_cur=True,
                        name="swa_attn_prompt")
    xp_new = _outproj_ln(og.reshape(bp * sp, qw), w_out, xp, ln_g, ln_b, alpha, TM_OUT, "swa_out_prompt")
    last = lambda a: a.reshape(bp, sp, kvw)[:, sp - WINDOW:].reshape(bp, WINDOW, kvh, hd)
    new_p = (last(kf), last(vf))
    g, q, kb, kf, vb, vf = _proj(xs, w_in, segs, TM_PROJ, "swa_proj_sample")
    og = _swa_attention(q.reshape(bs, ss, qw),
                        cache_k.reshape(bs, nprev, kvw).astype(BF16), kb.reshape(bs, ss, kvw),
                        cache_v.reshape(bs, nprev, kvw).astype(BF16), vb.reshape(bs, ss, kvw),
                        g.reshape(bs, ss, qw), sinks,
                        cq=ss, nch=1, nprev=nprev, prev_from_cur=False, name="swa_attn_sample")
    xs_new = _outproj_ln(og.reshape(bs * ss, qw), w_out, xs, ln_g, ln_b, alpha, TM_OUT, "swa_out_sample")
    new_s = (kf.reshape(bs, ss, kvh, hd), vf.reshape(bs, ss, kvh, hd))
    return xp_new, xs_new, new_p, new_s


def _s5_layer(xp, xs, bp, bs, state_re, state_im, w_in, a_re, a_im, log_dt, b_re, b_im, c_re, c_im,
              d_vec, w_glu, w_out, ln_g, ln_b, alpha):
    groups, p_ = a_re.shape
    c = groups * SSM_GROUP
    ns = c // LANES
    sg = SSM_SLAB_GROUPS
    L = SSM_L
    lw = L * LANES
    sp, ss = xp.shape[0] // bp, xs.shape[0] // bs
    assert sp % L == 0 and ss % L == 0
    w_in = w_in.astype(BF16)
    w_glu = w_glu.astype(BF16)
    w_out = w_out.astype(BF16)
    tables = _ssm_tables(a_re, a_im, log_dt, b_re, b_im, c_re, c_im, L)
    segs = ((0, c, 1.0, (("rows", BF16),)), (c, c, 1.0, (("plain", BF16),)))

    def split_state(hfin):
        re = hfin[..., :sg * p_].reshape(hfin.shape[:-1] + (sg, p_))
        im = hfin[..., sg * p_:].reshape(hfin.shape[:-1] + (sg, p_))
        return re, im

    u, g = _proj(xp, w_in, segs, TM_PROJ, "s5_proj_prompt")
    h0 = jnp.zeros((bp, ns, 1, 2 * sg * p_), F32)
    seqs_per_step = SUBLANES // (sg * p_ // LANES)
    if bp % seqs_per_step == 0:
        y, hfin = _ssm_dense(u.reshape(ns, bp, sp // L, lw), h0, tables,
                             tile_rows=SSM_TILE_ROWS, name="s5_ssm_prompt")
    else:
        y, hfin = _ssm(u.reshape(ns, bp, sp // L, lw), h0, tables,
                       nchains=1, tile_rows=SSM_TILE_ROWS, name="s5_ssm_prompt")
    xp_new = _s5_tail(y.reshape(ns, bp * sp // L, lw), u, g, d_vec, w_glu, w_out, xp, ln_g, ln_b, alpha,
                      TM_OUT, "s5_tail_prompt")
    re, im = split_state(hfin[:, :, 0])
    new_p = (re.reshape(bp, groups, p_), im.reshape(bp, groups, p_))

    u, g = _proj(xs, w_in, segs, TM_PROJ, "s5_proj_sample")
    nck = ss // L
    u_rows = u.reshape(ns, bs, nck, lw).transpose(0, 2, 1, 3).reshape(ns, 1, nck * bs, lw)
    st = jnp.concatenate([state_re.astype(F32).reshape(bs, ns, sg * p_),
                          state_im.astype(F32).reshape(bs, ns, sg * p_)], axis=-1)
    h0 = st.transpose(1, 0, 2)[None]
    y, hfin = _ssm(u_rows, h0, tables, nchains=bs, tile_rows=nck * bs, name="s5_ssm_sample")
    y = y.reshape(ns, nck, bs, lw).transpose(0, 2, 1, 3).reshape(ns, bs * nck, lw)
    xs_new = _s5_tail(y, u, g, d_vec, w_glu, w_out, xs, ln_g, ln_b, alpha, TM_OUT, "s5_tail_sample")
    re, im = split_state(hfin[0].transpose(1, 0, 2))
    new_s = (re.reshape(bs, groups, p_), im.reshape(bs, groups, p_))
    return xp_new, xs_new, new_p, new_s


def kernel(x_prompt, x_sample, cache_k_a, cache_v_a, cache_k_b, cache_v_b, state_re_c, state_im_c, ln_g, ln_b, w_in_a, w_out_a, w_in_b, sinks_b, w_out_b, w_in_c, a_re_c, a_im_c, log_dt_c, b_re_c, b_im_c, c_re_c, c_im_c, d_c, w_glu_c, w_out_c):
    depth = ln_g.shape[0]
    alpha = (2 * depth) ** 0.25
    bp, sp, d = x_prompt.shape
    bs, ss, _ = x_sample.shape
    xp = x_prompt.reshape(bp * sp, d)
    xs = x_sample.reshape(bs * ss, d)
    n_a = cache_k_a.shape[0]
    heads_a, hd_a = cache_k_a.shape[3], cache_k_a.shape[4]
    outs = {k: [] for k in ("ka_s", "va_s", "kb_p", "vb_p", "kb_s", "vb_s", "hr_p", "hi_p", "hr_s", "hi_s")}
    sb_stack = None
    for i in range(depth):
        j, kind = i // N_MIXERS, i % N_MIXERS
        if kind == 0:
            xp, xs, sb_stack, new_s = _sb_layer(xp, xs, bp, bs, cache_k_a, cache_v_a, w_in_a, w_out_a,
                                                ln_g[i], ln_b[i], alpha, (j, n_a, sb_stack))
            new_p, names = (), ("ka_s", "va_s")
        elif kind == 1:
            xp, xs, new_p, new_s = _swa_layer(xp, xs, bp, bs, cache_k_b[j], cache_v_b[j], w_in_b[j], sinks_b[j],
                                              w_out_b[j], ln_g[i], ln_b[i], alpha)
            names = ("kb_p", "vb_p", "kb_s", "vb_s")
        else:
            xp, xs, new_p, new_s = _s5_layer(xp, xs, bp, bs, state_re_c[j], state_im_c[j], w_in_c[j], a_re_c[j],
                                             a_im_c[j], log_dt_c[j], b_re_c[j], b_im_c[j], c_re_c[j], c_im_c[j],
                                             d_c[j], w_glu_c[j], w_out_c[j], ln_g[i], ln_b[i], alpha)
            names = ("hr_p", "hi_p", "hr_s", "hi_s")
        for nm, val in zip(names, tuple(new_p) + tuple(new_s)):
            outs[nm].append(val)
    ka_p = sb_stack[0].reshape(n_a, bp, sp, heads_a, hd_a)
    va_p = sb_stack[1].reshape(n_a, bp, sp, heads_a, hd_a)
    return (xp.reshape(bp, sp, d), xs.reshape(bs, ss, d),
            ka_p, va_p, jnp.stack(outs["ka_s"]), jnp.stack(outs["va_s"]),
            jnp.stack(outs["kb_p"]), jnp.stack(outs["vb_p"]), jnp.stack(outs["kb_s"]), jnp.stack(outs["vb_s"]),
            jnp.stack(outs["hr_p"]), jnp.stack(outs["hi_p"]), jnp.stack(outs["hr_s"]), jnp.stack(outs["hi_s"]))
```

```python
import functools
import math

import jax
import jax.numpy as jnp
from jax import lax
from jax.experimental import pallas as pl
from jax.experimental.pallas import tpu as pltpu

F32 = jnp.float32
BF16 = jnp.bfloat16

CHUNK = 64
WINDOW = 128
N_MIXERS = 3
SSM_GROUP = 16
LN_EPS = 1e-5
SB_KBLOCK = 128

LANES = 128
SUBLANES = 8
VMEM_LIMIT = 56 * 1024 * 1024

LOG2E = math.log2(math.e)
SB_DEAD_LOG2 = -127.0
SSM_L = SUBLANES
SSM_SLAB_GROUPS = LANES // SSM_GROUP


def _silu(x):
    return x * (1.0 / (1.0 + jnp.exp(-x)))


def _cparams(sem, vmem=VMEM_LIMIT):
    return pltpu.CompilerParams(dimension_semantics=sem, vmem_limit_bytes=vmem)


def _proj_kernel(*refs, segs, n_alias, step, stack_layer):
    x_ref, w_ref = refs[0], refs[1]
    out_refs = refs[2 + n_alias:]
    xb = x_ref[...].astype(BF16)
    outs = iter(out_refs)
    scr = out_refs[-1]
    for c0, width, scale, kinds in segs:
        o_refs = [next(outs) for _ in kinds]
        for s0 in range(0, width, step):
            s1 = min(width, s0 + step)
            acc = jnp.dot(xb, w_ref[:, c0 + s0:c0 + s1], preferred_element_type=F32)
            if scale != 1.0:
                acc = acc * scale
            for (kind, _), o_ref in zip(kinds, o_refs):
                if kind == "rows":
                    rows = acc.shape[0] // SUBLANES
                    for l0 in range(s0, s1, LANES):
                        slab = (l0 - s0) // LANES
                        scr[slab] = acc[:, l0 - s0:l0 - s0 + LANES]
                        for s in range(SUBLANES):
                            piece = scr[slab, pl.ds(s, rows, stride=SUBLANES), :]
                            o_ref[l0 // LANES, :, s * LANES:(s + 1) * LANES] = piece.astype(o_ref.dtype)
                elif kind == "stack_init":
                    for layer in range(o_ref.shape[0]):
                        val = acc.astype(o_ref.dtype) if layer == stack_layer else jnp.zeros(acc.shape, o_ref.dtype)
                        o_ref[layer, :, s0:s1] = val
                else:
                    o_ref[:, s0:s1] = acc.astype(o_ref.dtype)


def _resident_weight_spec(w, layer):
    if w.ndim == 3:
        return pl.BlockSpec((None,) + w.shape[1:], lambda i: (layer, 0, 0), pipeline_mode=pl.Buffered(1))
    return pl.BlockSpec(w.shape, lambda i: (0, 0), pipeline_mode=pl.Buffered(1))


def _proj(x2d, w_bf16, segs, tm, name, stack=None, w_layer=0):
    t, d = x2d.shape
    tm = min(tm, t)
    assert t % tm == 0 and tm % SUBLANES == 0
    step = 1024
    out_specs, out_shape, alias_in = [], [], []
    need_scratch = False
    for _, wd, _, kinds in segs:
        for kind, dt in kinds:
            if kind == "rows":
                need_scratch = True
                out_specs.append(pl.BlockSpec((wd // LANES, tm // SUBLANES, SUBLANES * LANES), lambda i: (0, i, 0)))
                out_shape.append(jax.ShapeDtypeStruct((wd // LANES, t // SUBLANES, SUBLANES * LANES), dt))
            elif kind == "stack_init":
                out_specs.append(pl.BlockSpec((stack[1], tm, wd), lambda i: (0, i, 0)))
                out_shape.append(jax.ShapeDtypeStruct((stack[1], t, wd), dt))
            elif kind == "stack":
                j, nl, prev = stack
                out_specs.append(pl.BlockSpec((None, tm, wd), functools.partial(lambda i, j_: (j_, i, 0), j_=j)))
                out_shape.append(jax.ShapeDtypeStruct((nl, t, wd), dt))
                alias_in.append((len(out_shape) - 1, prev[len(alias_in)]))
            else:
                out_specs.append(pl.BlockSpec((tm, wd), lambda i: (i, 0)))
                out_shape.append(jax.ShapeDtypeStruct((t, wd), dt))
    in_specs = [
        pl.BlockSpec((tm, d), lambda i: (i, 0)),
        _resident_weight_spec(w_bf16, w_layer),
    ] + [pl.BlockSpec(memory_space=pl.ANY) for _ in alias_in]
    scratch = [pltpu.VMEM((step // LANES, tm, LANES) if need_scratch else (1, SUBLANES, LANES), F32)]
    kern = functools.partial(_proj_kernel, segs=segs, n_alias=len(alias_in), step=step,
                             stack_layer=None if stack is None else stack[0])
    return pl.pallas_call(
        kern,
        grid=(t // tm,),
        in_specs=in_specs,
        out_specs=out_specs,
        out_shape=out_shape,
        scratch_shapes=scratch,
        input_output_aliases={2 + k: oi for k, (oi, _) in enumerate(alias_in)},
        compiler_params=_cparams(("parallel",)),
        name=name,
    )(x2d, w_bf16, *[buf for _, buf in alias_in])


def _layer_norm_rows(r, g, b):
    mu = jnp.mean(r, axis=-1, keepdims=True)
    rc = r - mu
    var = jnp.mean(rc * rc, axis=-1, keepdims=True)
    return rc * lax.rsqrt(var + LN_EPS) * g + b


def _outproj_kernel(a_ref, w_ref, x_ref, g_ref, b_ref, o_ref, *, alpha, sub):
    for r0 in range(0, a_ref.shape[0], sub):
        y = jnp.dot(a_ref[r0:r0 + sub, :], w_ref[...], preferred_element_type=F32)
        r = alpha * x_ref[r0:r0 + sub, :] + y
        o_ref[r0:r0 + sub, :] = _layer_norm_rows(r, g_ref[...], b_ref[...])


def _outproj_ln(a2d, w_bf16, x2d, ln_g, ln_b, alpha, tm, name, w_layer=0):
    t, wdt = a2d.shape
    d = x2d.shape[1]
    tm = min(tm, t)
    assert t % tm == 0
    return pl.pallas_call(
        functools.partial(_outproj_kernel, alpha=alpha, sub=min(tm, OUT_SUB_ROWS)),
        grid=(t // tm,),
        in_specs=[
            pl.BlockSpec((tm, wdt), lambda i: (i, 0)),
            _resident_weight_spec(w_bf16, w_layer),
            pl.BlockSpec((tm, d), lambda i: (i, 0)),
            pl.BlockSpec((1, d), lambda i: (0, 0)),
            pl.BlockSpec((1, d), lambda i: (0, 0)),
        ],
        out_specs=pl.BlockSpec((tm, d), lambda i: (i, 0)),
        out_shape=jax.ShapeDtypeStruct((t, d), F32),
        compiler_params=_cparams(("parallel",)),
        name=name,
    )(a2d, w_bf16, x2d, ln_g.reshape(1, d).astype(F32), ln_b.reshape(1, d).astype(F32))


def _scores(qs, kblks):
    return [lax.dot_general(q, k, (((1,), (1,)), ((), ())), preferred_element_type=F32) for q, k in zip(qs, kblks)]


def _sb_blocks(zs, vblks, masks, laters, accs, tri):
    n = len(zs)
    rows = zs[0].shape[0]
    lbetas, cats = [], []
    for c in range(n):
        z = zs[c]
        sp = jnp.maximum(z, 0.0) + jnp.log2(1.0 + jnp.exp2(-jnp.abs(z)))
        lbetas.append(z - sp)
        if masks[c] is not None:
            sp = jnp.where(masks[c], sp, 0.0)
        hi = sp.astype(BF16)
        lo = (sp - hi.astype(F32)).astype(BF16)
        cats.append(jnp.concatenate([hi, lo], axis=1))
    cs = jnp.dot(jnp.concatenate(cats, axis=0), tri, preferred_element_type=F32)
    new_l, ws = [], []
    for c in range(n):
        blk = cs[c * rows:(c + 1) * rows]
        w = jnp.exp2(lbetas[c] + blk[:, :SB_KBLOCK] + laters[c])
        if masks[c] is not None:
            w = jnp.where(masks[c], w, 0.0)
        ws.append(w.astype(BF16))
        new_l.append(laters[c] + blk[:, SB_KBLOCK:])
    new_a = [accs[c] + jnp.dot(ws[c], vblks[c], preferred_element_type=F32) for c in range(n)]
    return new_l, new_a


def _sb_alive(ls):
    m = ls[0]
    for l in ls[1:]:
        m = jnp.maximum(m, l)
    return jnp.max(m) > SB_DEAD_LOG2


def _sb_tri():
    j = jnp.arange(SB_KBLOCK)
    later = -(j[:, None] > j[None, :]).astype(BF16)
    ones = -jnp.ones((SB_KBLOCK, SB_KBLOCK), BF16)
    one = jnp.concatenate([later, ones], axis=1)
    return jnp.concatenate([one, one], axis=0)


def _sb_prompt_kernel(q_ref, k_ref, v_ref, g_ref, tri_ref, o_ref, *, nchain):
    kb = SB_KBLOCK
    i = pl.program_id(2)
    n0 = [i * nchain + c for c in range(nchain)]
    n0_max = n0[-1]
    col = lax.broadcasted_iota(jnp.int32, (kb, kb), 1)
    row = lax.broadcasted_iota(jnp.int32, (kb, kb), 0)

    def offs(nbs):
        return [pl.multiple_of(jnp.maximum(nb, 0) * kb, kb) for nb in nbs]

    def loadk(nbs):
        return [k_ref[0, pl.ds(o, kb), :] for o in offs(nbs)]

    def loadv(nbs):
        return [v_ref[0, pl.ds(o, kb), :] for o in offs(nbs)]

    def queries():
        return [q_ref[0, c * kb:(c + 1) * kb, :] for c in range(nchain)]

    zero = jnp.zeros((kb, kb), F32)
    diag = col < row
    laters, accs = _sb_blocks(_scores(queries(), loadk(n0)), loadv(n0), [diag] * nchain,
                              [zero] * nchain, [zero] * nchain, tri_ref[...])

    def cond(st):
        return jnp.logical_and(st[0] <= n0_max, st[1])

    def earlier_block(t, laters, accs):
        nbs = [n0[c] - t for c in range(nchain)]
        ls = [laters[c] + jnp.where(nbs[c] >= 0, 0.0, -1e30).astype(F32) for c in range(nchain)]
        return _sb_blocks(_scores(queries(), loadk(nbs)), loadv(nbs), [None] * nchain, ls, list(accs), tri_ref[...])

    laters, accs = earlier_block(1, laters, accs)

    def body(st):
        ls, as_ = earlier_block(st[0], st[2], st[3])
        return (st[0] + 1, _sb_alive(ls), tuple(ls), tuple(as_))

    st = lax.while_loop(cond, body, (jnp.int32(2), _sb_alive(laters), tuple(laters), tuple(accs)))
    accs = st[3]
    for c in range(nchain):
        g = g_ref[0, c * kb:(c + 1) * kb, :].astype(F32)
        o_ref[0, c * kb:(c + 1) * kb, :] = (accs[c] * _silu(g)).astype(o_ref.dtype)


def _sb_attention_prompt(q, k, v, g, heads, nchain, name):
    b, s, w = q.shape
    hd = w // heads
    tq = SB_KBLOCK * nchain
    assert hd == SB_KBLOCK and s % tq == 0
    qmap = lambda bi, h, i: (bi, i, h)
    kmap = lambda bi, h, i: (bi, 0, h)
    return pl.pallas_call(
        functools.partial(_sb_prompt_kernel, nchain=nchain),
        grid=(b, heads, s // tq),
        in_specs=[
            pl.BlockSpec((1, tq, hd), qmap),
            pl.BlockSpec((1, s, hd), kmap),
            pl.BlockSpec((1, s, hd), kmap),
            pl.BlockSpec((1, tq, hd), qmap),
            pl.BlockSpec((2 * SB_KBLOCK, 2 * SB_KBLOCK), lambda bi, h, i: (0, 0)),
        ],
        out_specs=pl.BlockSpec((1, tq, hd), qmap),
        out_shape=jax.ShapeDtypeStruct((b, s, w), BF16),
        compiler_params=_cparams(("parallel", "parallel", "arbitrary")),
        name=name,
    )(q, k, v, g, _sb_tri())


def _sb_sample_kernel(q_ref, kn_ref, vn_ref, g_ref, ck_ref, cv_ref, ckh_ref, cvh_ref, tri_ref, o_ref,
                      kscr, vscr, sem, *, heads, wblk, nrem, layer):
    kb = SB_KBLOCK
    b = pl.program_id(0)
    ss = q_ref.shape[1]
    qs = [q_ref[0, :, h * kb:(h + 1) * kb] for h in range(heads)]
    zpad = jnp.zeros((kb - ss, kb), BF16)
    col = lax.broadcasted_iota(jnp.int32, (ss, kb), 1)
    row = lax.broadcasted_iota(jnp.int32, (ss, kb), 0)
    new_mask = col < row
    kbl = [jnp.concatenate([kn_ref[0, :, h * kb:(h + 1) * kb], zpad], axis=0) for h in range(heads)]
    vbl = [jnp.concatenate([vn_ref[0, :, h * kb:(h + 1) * kb], zpad], axis=0) for h in range(heads)]
    zero = jnp.zeros((ss, kb), F32)
    laters, accs = _sb_blocks(_scores(qs, kbl), vbl, [new_mask] * heads, [zero] * heads, [zero] * heads, tri_ref[...])

    def head_blocks(ref, r0):
        return [ref[pl.ds(r0 + h, kb, stride=heads), :].astype(BF16) for h in range(heads)]

    for j in range(wblk):
        r0 = (wblk - 1 - j) * kb * heads
        laters, accs = _sb_blocks(_scores(qs, head_blocks(ck_ref.at[0], r0)), head_blocks(cv_ref.at[0], r0),
                                  [None] * heads, laters, accs, tri_ref[...])

    if nrem > 0:
        def cond(st):
            return jnp.logical_and(st[0] < nrem, st[1])

        def body(st):
            t = st[0]
            r0 = pl.multiple_of((nrem - 1 - t) * (kb * heads), kb * heads)
            ck = pltpu.make_async_copy(ckh_ref.at[layer, b, pl.ds(r0, kb * heads)], kscr, sem.at[0])
            cv = pltpu.make_async_copy(cvh_ref.at[layer, b, pl.ds(r0, kb * heads)], vscr, sem.at[1])
            ck.start()
            cv.start()
            ck.wait()
            cv.wait()
            ls, as_ = _sb_blocks(_scores(qs, head_blocks(kscr, 0)), head_blocks(vscr, 0), [None] * heads,
                                 list(st[2]), list(st[3]), tri_ref[...])
            return (t + 1, _sb_alive(ls), tuple(ls), tuple(as_))

        st = lax.while_loop(cond, body, (jnp.int32(0), _sb_alive(laters), tuple(laters), tuple(accs)))
        accs = st[3]
    for h in range(heads):
        g = g_ref[0, :, h * kb:(h + 1) * kb].astype(F32)
        o_ref[0, :, h * kb:(h + 1) * kb] = (accs[h] * _silu(g)).astype(o_ref.dtype)


def _sb_attention_sample(q, k_new, v_new, g, cache_k, cache_v, layer, wblk, name):
    b, ss, w = q.shape
    nl, _, past, heads, hd = cache_k.shape
    assert hd == SB_KBLOCK and w == heads * hd and ss <= SB_KBLOCK and ss % 16 == 0
    wblk = min(wblk, past // SB_KBLOCK)
    tail = wblk * SB_KBLOCK
    assert past % tail == 0
    nrem = past // SB_KBLOCK - wblk
    ck = cache_k.reshape(nl, b, past * heads, hd)
    cv = cache_v.reshape(nl, b, past * heads, hd)
    tok = lambda bi: (bi, 0, 0)
    tail_map = lambda bi: (layer, bi, past // tail - 1, 0)
    kern = functools.partial(_sb_sample_kernel, heads=heads, wblk=wblk, nrem=nrem, layer=layer)
    return pl.pallas_call(
        kern,
        grid=(b,),
        in_specs=[
            pl.BlockSpec((1, ss, w), tok),
            pl.BlockSpec((1, ss, w), tok),
            pl.BlockSpec((1, ss, w), tok),
            pl.BlockSpec((1, ss, w), tok),
            pl.BlockSpec((None, 1, tail * heads, hd), tail_map),
            pl.BlockSpec((None, 1, tail * heads, hd), tail_map),
            pl.BlockSpec(memory_space=pl.ANY),
            pl.BlockSpec(memory_space=pl.ANY),
            pl.BlockSpec((2 * SB_KBLOCK, 2 * SB_KBLOCK), lambda bi: (0, 0)),
        ],
        out_specs=pl.BlockSpec((1, ss, w), tok),
        out_shape=jax.ShapeDtypeStruct((b, ss, w), BF16),
        scratch_shapes=[pltpu.VMEM((SB_KBLOCK * heads, hd), F32), pltpu.VMEM((SB_KBLOCK * heads, hd), F32),
                        pltpu.SemaphoreType.DMA((2,))],
        compiler_params=_cparams(("arbitrary",)),
        name=name,
    )(q, k_new, v_new, g, ck, cv, ck, cv, _sb_tri())


def _swa_kernel(q_ref, kp_ref, kc_ref, vp_ref, vc_ref, g_ref, bias_ref, sink_ref, o_ref,
                *, cq, nch, span, kvh, rep, hd, mask_first):
    i = pl.program_id(1)
    kbuf = jnp.concatenate([kp_ref[0], kc_ref[0]], axis=0)
    vbuf = jnp.concatenate([vp_ref[0], vc_ref[0]], axis=0)
    nprev = kp_ref.shape[1]
    gw = rep * hd
    for kv in range(kvh):
        kk = kbuf[:, kv * hd:(kv + 1) * hd]
        vv = vbuf[:, kv * hd:(kv + 1) * hd]
        bias = bias_ref[kv]
        sink = sink_ref[kv]
        sts = []
        for j in range(nch):
            ks = kk[j * cq:j * cq + span]
            qs = q_ref[0, j * cq:(j + 1) * cq, kv * gw:(kv + 1) * gw]
            lhs = jnp.concatenate([qs[:, r * hd:(r + 1) * hd] for r in range(rep)], axis=0)
            st = lax.dot_general(ks, lhs, (((1,), (1,)), ((), ())), preferred_element_type=F32) + bias
            if mask_first and j * cq < nprev:
                buf_row = j * cq + lax.broadcasted_iota(jnp.int32, st.shape, 0)
                valid = jnp.logical_or(i > 0, buf_row >= nprev)
                st = jnp.where(valid, st, -jnp.inf)
            sts.append(st)
        ps = []
        for st in sts:
            m = jnp.maximum(jnp.max(st, axis=0, keepdims=True), sink)
            p = jnp.exp2(st - m)
            den = jnp.sum(p, axis=0, keepdims=True) + jnp.exp2(sink - m)
            ps.append((p * (1.0 / den)).astype(BF16))
        for j in range(nch):
            vs = vv[j * cq:j * cq + span]
            o = lax.dot_general(ps[j], vs, (((0,), (0,)), ((), ())), preferred_element_type=F32)
            ot = jnp.concatenate([o[r * cq:(r + 1) * cq] for r in range(rep)], axis=1)
            g = g_ref[0, j * cq:(j + 1) * cq, kv * gw:(kv + 1) * gw].astype(F32)
            o_ref[0, j * cq:(j + 1) * cq, kv * gw:(kv + 1) * gw] = (ot * _silu(g)).astype(o_ref.dtype)


def _alibi_bias_t(kvh, rep, cq, span, nprev):
    n = kvh * rep
    slopes = 2.0 ** (-8.0 * jnp.arange(1, n + 1, dtype=F32) / n)
    qi = jnp.arange(cq, dtype=jnp.int32)
    kj = jnp.arange(span, dtype=jnp.int32)
    dist = jnp.abs(qi[None, :] + nprev - kj[:, None]).astype(F32)
    bias = -(slopes * LOG2E).reshape(kvh, 1, rep, 1) * dist[None, :, None, :]
    return bias.reshape(kvh, span, rep * cq)


def _swa_attention(q, k_prev_src, k_cur, v_prev_src, v_cur, g, sinks, *, cq, nch, nprev, prev_from_cur, name):
    b, sq, qw = q.shape
    kvw = k_cur.shape[2]
    kvh, rep = sinks.shape
    hd = kvw // kvh
    rows = cq * nch
    span = nprev + cq
    assert sq % rows == 0 and qw == kvh * rep * hd
    if prev_from_cur:
        assert rows % nprev == 0
        ratio = rows // nprev
        prev_map = lambda bi, i: (bi, jnp.maximum(i * ratio - 1, 0), 0)
    else:
        assert sq == rows
        prev_map = lambda bi, i: (bi, 0, 0)
    cur_map = lambda bi, i: (bi, i, 0)
    const3 = lambda bi, i: (0, 0, 0)
    sink_t = jnp.broadcast_to((sinks.astype(F32) * LOG2E)[:, None, :, None], (kvh, 1, rep, cq)).reshape(kvh, 1, rep * cq)
    kern = functools.partial(_swa_kernel, cq=cq, nch=nch, span=span, kvh=kvh, rep=rep, hd=hd,
                             mask_first=prev_from_cur)
    return pl.pallas_call(
        kern,
        grid=(b, sq // rows),
        in_specs=[
            pl.BlockSpec((1, rows, qw), cur_map),
            pl.BlockSpec((1, nprev, kvw), prev_map),
            pl.BlockSpec((1, rows, kvw), cur_map),
            pl.BlockSpec((1, nprev, kvw), prev_map),
            pl.BlockSpec((1, rows, kvw), cur_map),
            pl.BlockSpec((1, rows, qw), cur_map),
            pl.BlockSpec((kvh, span, rep * cq), const3),
            pl.BlockSpec((kvh, 1, rep * cq), const3),
        ],
        out_specs=pl.BlockSpec((1, rows, qw), cur_map),
        out_shape=jax.ShapeDtypeStruct((b, sq, qw), BF16),
        compiler_params=_cparams(("parallel", "arbitrary")),
        name=name,
    )(q, k_prev_src, k_cur, v_prev_src, v_cur, g, _alibi_bias_t(kvh, rep, cq, span, nprev), sink_t)


def _ssm_kernel(u_ref, w_ref, m_ref, v_ref, lam_ref, d_ref, h0_ref, y_ref, hfin_ref, s_scr, h_scr, carry,
                *, nchains):
    ti = pl.program_id(2)
    nrow = s_scr.shape[0]
    half = s_scr.shape[1] // 2
    steps = nrow // nchains

    @pl.when(ti == 0)
    def _():
        carry[...] = h0_ref[0, 0]

    u = u_ref[0, 0]
    s_scr[...] = jnp.dot(u, w_ref[0], preferred_element_type=F32)

    lam = lam_ref[0]
    lr = jnp.broadcast_to(lam[:, :half], (nchains, half))
    li = jnp.broadcast_to(lam[:, half:], (nchains, half))

    def step(c, h):
        r0 = pl.multiple_of(c * nchains, nchains)
        h_scr[pl.ds(r0, nchains), :] = h
        s = s_scr[pl.ds(r0, nchains), :]
        hr, hi = h[:, :half], h[:, half:]
        nr = lr * hr - li * hi + s[:, :half]
        ni = lr * hi + li * hr + s[:, half:]
        return jnp.concatenate([nr, ni], axis=1)

    h = lax.fori_loop(0, steps, step, carry[...], unroll=(8 if steps % 8 == 0 else 1))
    carry[...] = h
    hfin_ref[0, 0] = h

    y = (jnp.dot(u, m_ref[0], preferred_element_type=F32)
         + jnp.dot(h_scr[...].astype(BF16), v_ref[0], preferred_element_type=F32))
    y_ref[0, 0] = _gelu_tanh(y + d_ref[0] * u.astype(F32))


def _ssm_dense_kernel(u_ref, w_ref, m_ref, v_ref, lam_ref, d_ref, h0_ref, y_ref, hfin_ref,
                      sre, sim, hre, him, cre, cim, *, nb, kt):
    ti = pl.program_id(2)
    rows = u_ref.shape[2]
    half = kt * LANES

    def fold(row_of):
        return jnp.concatenate([row_of(b, k) for b in range(nb) for k in range(kt)], axis=0)

    @pl.when(ti == 0)
    def _():
        cre[...] = fold(lambda b, k: h0_ref[b, 0, :, k * LANES:(k + 1) * LANES])
        cim[...] = fold(lambda b, k: h0_ref[b, 0, :, half + k * LANES:half + (k + 1) * LANES])

    us = [u_ref[0, b] for b in range(nb)]
    for b in range(nb):
        s = jnp.dot(us[b], w_ref[0], preferred_element_type=F32)
        for k in range(kt):
            sre[pl.ds(b * kt + k, rows, stride=SUBLANES), :] = s[:, k * LANES:(k + 1) * LANES]
            sim[pl.ds(b * kt + k, rows, stride=SUBLANES), :] = s[:, half + k * LANES:half + (k + 1) * LANES]

    lam = lam_ref[0]
    lr = fold(lambda b, k: lam[:, k * LANES:(k + 1) * LANES])
    li = fold(lambda b, k: lam[:, half + k * LANES:half + (k + 1) * LANES])

    def step(c, carry):
        hr, hi = carry
        r0 = pl.multiple_of(c * SUBLANES, SUBLANES)
        hre[pl.ds(r0, SUBLANES), :] = hr
        him[pl.ds(r0, SUBLANES), :] = hi
        nr = lr * hr - li * hi + sre[pl.ds(r0, SUBLANES), :]
        ni = lr * hi + li * hr + sim[pl.ds(r0, SUBLANES), :]
        return nr, ni

    hr, hi = lax.fori_loop(0, rows, step, (cre[...], cim[...]), unroll=(8 if rows % 8 == 0 else 1))
    cre[...] = hr
    cim[...] = hi
    for b in range(nb):
        pieces = ([hr[b * kt + k:b * kt + k + 1] for k in range(kt)]
                  + [hi[b * kt + k:b * kt + k + 1] for k in range(kt)])
        hfin_ref[b, 0] = jnp.concatenate(pieces, axis=1)
        hcat = jnp.concatenate(
            [hre[pl.ds(b * kt + k, rows, stride=SUBLANES), :] for k in range(kt)]
            + [him[pl.ds(b * kt + k, rows, stride=SUBLANES), :] for k in range(kt)], axis=1)
        yv = jnp.dot(hcat.astype(BF16), v_ref[0], preferred_element_type=F32)
        cw = 2 * LANES
        for c0 in range(0, m_ref.shape[2], cw):
            y = yv[:, c0:c0 + cw] + jnp.dot(
                us[b][:, :c0 + cw], m_ref[0, :c0 + cw, c0:c0 + cw], preferred_element_type=F32)
            skip = d_ref[0][:, c0:c0 + cw] * us[b][:, c0:c0 + cw].astype(F32)
            y_ref[0, b, :, c0:c0 + cw] = _gelu_tanh(y + skip)


def _ssm_dense(u_rows, h0, tables, *, tile_rows, name):
    w, m, v, lam, d_rows = tables
    ns, b, r, lw = u_rows.shape
    sw = w.shape[2]
    kt = sw // (2 * LANES)
    nb = SUBLANES // kt
    tile_rows = min(tile_rows, r)
    assert r % tile_rows == 0 and b % nb == 0 and nb * kt == SUBLANES
    row_map = lambda o, bi, ti: (o, bi, ti, 0)
    tab = lambda o, bi, ti: (o, 0, 0)
    st_map = lambda o, bi, ti: (bi, o, 0, 0)
    fold_scr = pltpu.VMEM((tile_rows * SUBLANES, LANES), F32)
    return pl.pallas_call(
        functools.partial(_ssm_dense_kernel, nb=nb, kt=kt),
        grid=(ns, b // nb, r // tile_rows),
        in_specs=[
            pl.BlockSpec((1, nb, tile_rows, lw), row_map),
            pl.BlockSpec((1, lw, sw), tab),
            pl.BlockSpec((1, lw, lw), tab),
            pl.BlockSpec((1, sw, lw), tab),
            pl.BlockSpec((1, 1, sw), tab),
            pl.BlockSpec((1, 1, lw), tab),
            pl.BlockSpec((nb, 1, 1, sw), st_map),
        ],
        out_specs=[
            pl.BlockSpec((1, nb, tile_rows, lw), row_map),
            pl.BlockSpec((nb, 1, 1, sw), st_map),
        ],
        out_shape=[jax.ShapeDtypeStruct((ns, b, r, lw), F32),
                   jax.ShapeDtypeStruct((b, ns, 1, sw), F32)],
        scratch_shapes=[fold_scr, fold_scr, fold_scr, fold_scr,
                        pltpu.VMEM((SUBLANES, LANES), F32), pltpu.VMEM((SUBLANES, LANES), F32)],
        compiler_params=_cparams(("parallel", "parallel", "arbitrary")),
        name=name,
    )(u_rows, w, m, v, lam, d_rows, h0)


def _ssm_tables(a_re, a_im, log_dt, b_re, b_im, c_re, c_im, L):
    g_, p_ = a_re.shape
    sg = SSM_SLAB_GROUPS
    ns = g_ // sg
    hp = lax.Precision.HIGHEST
    a_re = jnp.minimum(a_re.astype(F32), -1e-4)
    a_im = a_im.astype(F32)
    dt = jnp.exp(log_dt.astype(F32))[:, None]
    mag = jnp.exp(dt * a_re)
    lb_re = mag * jnp.cos(dt * a_im)
    lb_im = mag * jnp.sin(dt * a_im)
    n_re = lb_re - 1.0
    den = a_re * a_re + a_im * a_im
    f_re = ((n_re * a_re + lb_im * a_im) / den)[..., None]
    f_im = ((lb_im * a_re - n_re * a_im) / den)[..., None]
    br, bi = b_re.astype(F32), b_im.astype(F32)
    bb_re = f_re * br - f_im * bi
    bb_im = f_re * bi + f_im * br
    cr, ci = c_re.astype(F32), c_im.astype(F32)
    tau = jnp.arange(L + 1, dtype=F32)[:, None, None]
    pmag = jnp.exp(tau * dt * a_re)
    pw_re = pmag * jnp.cos(tau * dt * a_im)
    pw_im = pmag * jnp.sin(tau * dt * a_im)

    lb_r = pw_re[:, :, :, None] * bb_re[None] - pw_im[:, :, :, None] * bb_im[None]
    lb_i = pw_re[:, :, :, None] * bb_im[None] + pw_im[:, :, :, None] * bb_re[None]
    kt = (jnp.einsum('gop,tgpi->tgoi', cr, lb_r[:L], precision=hp)
          - jnp.einsum('gop,tgpi->tgoi', ci, lb_i[:L], precision=hp))
    lag = jnp.arange(L)[None, :] - jnp.arange(L)[:, None]
    kst = jnp.where((lag >= 0)[:, :, None, None, None], kt[jnp.clip(lag, 0, L - 1)], 0.0)

    rows_u = L * LANES
    cols_s = 2 * sg * p_
    rid = jnp.arange(rows_u)
    grp_u = (rid % LANES) // SSM_GROUP
    sid = jnp.arange(cols_s)
    grp_s = (sid % (sg * p_)) // p_

    def expand(compact, rep_mat, grp_r, grp_c):
        full = jnp.einsum('nrc,ck->nrk', compact.astype(BF16), rep_mat.astype(BF16),
                          preferred_element_type=BF16)
        return jnp.where(grp_r[:, None] == grp_c[None, :], full, jnp.zeros((), BF16))

    m_c = kst.reshape(L, L, ns, sg, SSM_GROUP, SSM_GROUP).transpose(2, 0, 3, 5, 1, 4).reshape(ns, rows_u, L * SSM_GROUP)
    cid = jnp.arange(rows_u)
    rep_to = ((cid[None, :] // LANES == jnp.arange(L * SSM_GROUP)[:, None] // SSM_GROUP)
              & (cid[None, :] % SSM_GROUP == jnp.arange(L * SSM_GROUP)[:, None] % SSM_GROUP)).astype(F32)
    m = expand(m_c, rep_to, grp_u, grp_u)
    wri = jnp.stack([lb_r[:L][::-1], lb_i[:L][::-1]], axis=0).reshape(2, L, ns, sg, p_, SSM_GROUP)
    w_c = wri.transpose(2, 1, 3, 5, 0, 4).reshape(ns, rows_u, 2 * p_)
    rep_ap = ((sid[None, :] // (sg * p_) == jnp.arange(2 * p_)[:, None] // p_)
              & (sid[None, :] % p_ == jnp.arange(2 * p_)[:, None] % p_)).astype(F32)
    w = expand(w_c, rep_ap, grp_u, grp_s)
    cl_r = cr[None] * pw_re[1:][:, :, None, :] - ci[None] * pw_im[1:][:, :, None, :]
    cl_i = cr[None] * pw_im[1:][:, :, None, :] + ci[None] * pw_re[1:][:, :, None, :]
    vri = jnp.stack([cl_r, -cl_i], axis=0).reshape(2, L, ns, sg, SSM_GROUP, p_)
    v_c = vri.transpose(2, 0, 3, 5, 1, 4).reshape(ns, cols_s, L * SSM_GROUP)
    v = expand(v_c, rep_to, grp_s, grp_u)
    lam = jnp.stack([pw_re[L], pw_im[L]], axis=0).reshape(2, ns, sg * p_)
    lam = jnp.transpose(lam, (1, 0, 2)).reshape(ns, 1, cols_s)
    return w, m, v, lam


def _ssm(u_rows, h0, tables, *, nchains, tile_rows, name):
    w, m, v, lam, d_rows = tables
    ns, b, r, lw = u_rows.shape
    sw = w.shape[2]
    tile_rows = min(tile_rows, r)
    assert r % tile_rows == 0 and tile_rows % nchains == 0
    row_map = lambda o, bi, ti: (o, bi, ti, 0)
    tab = lambda o, bi, ti: (o, 0, 0)
    st_map = lambda o, bi, ti: (bi, o, 0, 0)
    return pl.pallas_call(
        functools.partial(_ssm_kernel, nchains=nchains),
        grid=(ns, b, r // tile_rows),
        in_specs=[
            pl.BlockSpec((1, 1, tile_rows, lw), row_map),
            pl.BlockSpec((1, lw, sw), tab),
            pl.BlockSpec((1, lw, lw), tab),
            pl.BlockSpec((1, sw, lw), tab),
            pl.BlockSpec((1, 1, sw), tab),
            pl.BlockSpec((1, 1, lw), tab),
            pl.BlockSpec((1, 1, nchains, sw), st_map),
        ],
        out_specs=[
            pl.BlockSpec((1, 1, tile_rows, lw), row_map),
            pl.BlockSpec((1, 1, nchains, sw), st_map),
        ],
        out_shape=[jax.ShapeDtypeStruct((ns, b, r, lw), F32),
                   jax.ShapeDtypeStruct((b, ns, nchains, sw), F32)],
        scratch_shapes=[pltpu.VMEM((tile_rows, sw), F32), pltpu.VMEM((tile_rows, sw), F32),
                        pltpu.VMEM((nchains, sw), F32)],
        compiler_params=_cparams(("parallel", "parallel", "arbitrary")),
        name=name,
    )(u_rows, w, m, v, lam, d_rows, h0)


def _gelu_tanh(x):
    return 0.5 * x * (1.0 + jnp.tanh(math.sqrt(2.0 / math.pi) * (x + 0.044715 * (x * x * x))))


def _s5_tail_kernel(y_ref, g_ref, wg_ref, wo_ref, x_ref, lg_ref, lb_ref, o_ref, y_scr, *, alpha):
    ns, rows, _ = y_ref.shape
    for o in range(ns):
        for s in range(SUBLANES):
            y_scr[o, pl.ds(s, rows, stride=SUBLANES), :] = y_ref[o, :, s * LANES:(s + 1) * LANES]
    y = jnp.concatenate([y_scr[o] for o in range(ns)], axis=1)
    z = jnp.dot(y.astype(BF16), wg_ref[...], preferred_element_type=F32)
    y = y * (1.0 / (1.0 + jnp.exp(-z)))
    a = (y * _silu(g_ref[...].astype(F32))).astype(BF16)
    out = jnp.dot(a, wo_ref[...], preferred_element_type=F32)
    r = alpha * x_ref[...] + out
    o_ref[...] = _layer_norm_rows(r, lg_ref[...], lb_ref[...])


def _s5_tail(y_rows, g2d, w_glu, w_out, x2d, ln_g, ln_b, alpha, tm, name):
    ns, tr, lw = y_rows.shape
    t = tr * SUBLANES
    c = ns * LANES
    d = x2d.shape[1]
    tm = min(tm, t)
    assert t % tm == 0 and tm % SUBLANES == 0 and lw == SUBLANES * LANES
    row_spec = pl.BlockSpec((ns, tm // SUBLANES, lw), lambda i: (0, i, 0))
    const = lambda i: (0, 0)
    return pl.pallas_call(
        functools.partial(_s5_tail_kernel, alpha=alpha),
        grid=(t // tm,),
        in_specs=[
            row_spec,
            pl.BlockSpec((tm, c), lambda i: (i, 0)),
            pl.BlockSpec((c, c), const, pipeline_mode=pl.Buffered(1)),
            pl.BlockSpec((c, d), const, pipeline_mode=pl.Buffered(1)),
            pl.BlockSpec((tm, d), lambda i: (i, 0)),
            pl.BlockSpec((1, d), const),
            pl.BlockSpec((1, d), const),
        ],
        out_specs=pl.BlockSpec((tm, d), lambda i: (i, 0)),
        out_shape=jax.ShapeDtypeStruct((t, d), F32),
        scratch_shapes=[pltpu.VMEM((ns, tm, LANES), F32)],
        compiler_params=_cparams(("parallel",)),
        name=name,
    )(y_rows, g2d, w_glu, w_out, x2d, ln_g.reshape(1, d).astype(F32), ln_b.reshape(1, d).astype(F32))


TM_PROJ = 512
TM_OUT = 512
TM_OUT_SB = 1024
OUT_SUB_ROWS = 256
SB_CHAINS = 16
SB_SAMPLE_TAIL_BLOCKS = 2
SWA_CHUNKS_PER_STEP = 4
SSM_TILE_ROWS = 512


def _sb_layer(xp, xs, bp, bs, cache_k, cache_v, w_in, w_out, ln_g, ln_b, alpha, stack_p):
    layer = stack_p[0]
    heads, hd = cache_k.shape[3], cache_k.shape[4]
    width = heads * hd
    sp, ss = xp.shape[0] // bp, xs.shape[0] // bs
    w_in = w_in.astype(BF16)
    w_out = w_out.astype(BF16)

    def segs(kv_kind):
        return ((3 * width, width, 1.0, (("plain", BF16),)),
                (0, width, LOG2E * hd ** -0.5, (("plain", BF16),)),
                (width, width, 1.0, (("plain", BF16), (kv_kind, F32))),
                (2 * width, width, 1.0, (("plain", BF16), (kv_kind, F32))))

    r3 = lambda a, b_, s_: a.reshape(b_, s_, width)
    kv_kind = "stack_init" if stack_p[2] is None else "stack"
    g, q, kb, kf, vb, vf = _proj(xp, w_in, segs(kv_kind), TM_PROJ, "sb_proj_prompt", stack=stack_p, w_layer=layer)
    og = _sb_attention_prompt(r3(q, bp, sp), r3(kb, bp, sp), r3(vb, bp, sp), r3(g, bp, sp), heads,
                              nchain=min(SB_CHAINS, sp // SB_KBLOCK), name="sb_attn_prompt")
    xp_new = _outproj_ln(og.reshape(bp * sp, width), w_out, xp, ln_g, ln_b, alpha, TM_OUT_SB, "sb_out_prompt",
                         w_layer=layer)
    g, q, kb, kf_s, vb, vf_s = _proj(xs, w_in, segs("plain"), TM_PROJ, "sb_proj_sample", w_layer=layer)
    og = _sb_attention_sample(r3(q, bs, ss), r3(kb, bs, ss), r3(vb, bs, ss), r3(g, bs, ss), cache_k, cache_v,
                              layer, SB_SAMPLE_TAIL_BLOCKS, "sb_attn_sample")
    xs_new = _outproj_ln(og.reshape(bs * ss, width), w_out, xs, ln_g, ln_b, alpha, TM_OUT_SB, "sb_out_sample",
                         w_layer=layer)
    new_s = (kf_s.reshape(bs, ss, heads, hd), vf_s.reshape(bs, ss, heads, hd))
    return xp_new, xs_new, (kf, vf), new_s


def _swa_layer(xp, xs, bp, bs, cache_k, cache_v, w_in, sinks, w_out, ln_g, ln_b, alpha):
    kvh, hd = cache_k.shape[2], cache_k.shape[3]
    rep = sinks.shape[1]
    qw, kvw = kvh * rep * hd, kvh * hd
    nprev = cache_k.shape[1]
    sp, ss = xp.shape[0] // bp, xs.shape[0] // bs
    assert nprev == WINDOW and sp >= WINDOW
    w_in = w_in.astype(BF16)
    w_out = w_out.astype(BF16)
    segs = ((qw + 2 * kvw, qw, 1.0, (("plain", BF16),)),
            (0, qw, LOG2E * hd ** -0.5, (("plain", BF16),)),
            (qw, kvw, 1.0, (("plain", BF16), ("plain", F32))),
            (qw + kvw, kvw, 1.0, (("plain", BF16), ("plain", F32))))
    g, q, kb, kf, vb, vf = _proj(xp, w_in, segs, TM_PROJ, "swa_proj_prompt")
    k3, v3 = kb.reshape(bp, sp, kvw), vb.reshape(bp, sp, kvw)
    og = _swa_attention(q.reshape(bp, sp, qw), k3, k3, v3, v3, g.reshape(bp, sp, qw), sinks,
                        cq=CHUNK, nch=SWA_CHUNKS_PER_STEP, nprev=WINDOW, prev_from_cur=True,
                        name="swa_attn_prompt")
    xp_new = _outproj_ln(og.reshape(bp * sp, qw), w_out, xp, ln_g, ln_b, alpha, TM_OUT, "swa_out_prompt")
    last = lambda a: a.reshape(bp, sp, kvw)[:, sp - WINDOW:].reshape(bp, WINDOW, kvh, hd)
    new_p = (last(kf), last(vf))
    g, q, kb, kf, vb, vf = _proj(xs, w_in, segs, TM_PROJ, "swa_proj_sample")
    og = _swa_attention(q.reshape(bs, ss, qw),
                        cache_k.reshape(bs, nprev, kvw).astype(BF16), kb.reshape(bs, ss, kvw),
                        cache_v.reshape(bs, nprev, kvw).astype(BF16), vb.reshape(bs, ss, kvw),
                        g.reshape(bs, ss, qw), sinks,
                        cq=ss, nch=1, nprev=nprev, prev_from_cur=False, name="swa_attn_sample")
    xs_new = _outproj_ln(og.reshape(bs * ss, qw), w_out, xs, ln_g, ln_b, alpha, TM_OUT, "swa_out_sample")
    new_s = (kf.reshape(bs, ss, kvh, hd), vf.reshape(bs, ss, kvh, hd))
    return xp_new, xs_new, new_p, new_s


def _s5_layer(xp, xs, bp, bs, state_re, state_im, w_in, a_re, a_im, log_dt, b_re, b_im, c_re, c_im,
              d_vec, w_glu, w_out, ln_g, ln_b, alpha):
    groups, p_ = a_re.shape
    c = groups * SSM_GROUP
    ns = c // LANES
    sg = SSM_SLAB_GROUPS
    L = SSM_L
    lw = L * LANES
    sp, ss = xp.shape[0] // bp, xs.shape[0] // bs
    assert sp % L == 0 and ss % L == 0
    w_in = w_in.astype(BF16)
    w_glu = w_glu.astype(BF16)
    w_out = w_out.astype(BF16)
    d_rows = jnp.tile(d_vec.astype(F32).reshape(ns, 1, LANES), (1, 1, L))
    tables = _ssm_tables(a_re, a_im, log_dt, b_re, b_im, c_re, c_im, L) + (d_rows,)
    segs = ((0, c, 1.0, (("rows", BF16),)), (c, c, 1.0, (("plain", BF16),)))

    def split_state(hfin):
        re = hfin[..., :sg * p_].reshape(hfin.shape[:-1] + (sg, p_))
        im = hfin[..., sg * p_:].reshape(hfin.shape[:-1] + (sg, p_))
        return re, im

    u, g = _proj(xp, w_in, segs, TM_PROJ, "s5_proj_prompt")
    h0 = jnp.zeros((bp, ns, 1, 2 * sg * p_), F32)
    seqs_per_step = SUBLANES // (sg * p_ // LANES)
    if bp % seqs_per_step == 0:
        y, hfin = _ssm_dense(u.reshape(ns, bp, sp // L, lw), h0, tables,
                             tile_rows=SSM_TILE_ROWS, name="s5_ssm_prompt")
    else:
        y, hfin = _ssm(u.reshape(ns, bp, sp // L, lw), h0, tables,
                       nchains=1, tile_rows=SSM_TILE_ROWS, name="s5_ssm_prompt")
    xp_new = _s5_tail(y.reshape(ns, bp * sp // L, lw), g, w_glu, w_out, xp, ln_g, ln_b, alpha,
                      TM_OUT, "s5_tail_prompt")
    re, im = split_state(hfin[:, :, 0])
    new_p = (re.reshape(bp, groups, p_), im.reshape(bp, groups, p_))

    u, g = _proj(xs, w_in, segs, TM_PROJ, "s5_proj_sample")
    nck = ss // L
    u_rows = u.reshape(ns, bs, nck, lw).transpose(0, 2, 1, 3).reshape(ns, 1, nck * bs, lw)
    st = jnp.concatenate([state_re.astype(F32).reshape(bs, ns, sg * p_),
                          state_im.astype(F32).reshape(bs, ns, sg * p_)], axis=-1)
    h0 = st.transpose(1, 0, 2)[None]
    y, hfin = _ssm(u_rows, h0, tables, nchains=bs, tile_rows=nck * bs, name="s5_ssm_sample")
    y = y.reshape(ns, nck, bs, lw).transpose(0, 2, 1, 3).reshape(ns, bs * nck, lw)
    xs_new = _s5_tail(y, g, w_glu, w_out, xs, ln_g, ln_b, alpha, TM_OUT, "s5_tail_sample")
    re, im = split_state(hfin[0].transpose(1, 0, 2))
    new_s = (re.reshape(bs, groups, p_), im.reshape(bs, groups, p_))
    return xp_new, xs_new, new_p, new_s


def kernel(x_prompt, x_sample, cache_k_a, cache_v_a, cache_k_b, cache_v_b, state_re_c, state_im_c, ln_g, ln_b, w_in_a, w_out_a, w_in_b, sinks_b, w_out_b, w_in_c, a_re_c, a_im_c, log_dt_c, b_re_c, b_im_c, c_re_c, c_im_c, d_c, w_glu_c, w_out_c):
    depth = ln_g.shape[0]
    alpha = (2 * depth) ** 0.25
    bp, sp, d = x_prompt.shape
    bs, ss, _ = x_sample.shape
    xp = x_prompt.reshape(bp * sp, d)
    xs = x_sample.reshape(bs * ss, d)
    n_a = cache_k_a.shape[0]
    heads_a, hd_a = cache_k_a.shape[3], cache_k_a.shape[4]
    outs = {k: [] for k in ("ka_s", "va_s", "kb_p", "vb_p", "kb_s", "vb_s", "hr_p", "hi_p", "hr_s", "hi_s")}
    sb_stack = None
    for i in range(depth):
        j, kind = i // N_MIXERS, i % N_MIXERS
        if kind == 0:
            xp, xs, sb_stack, new_s = _sb_layer(xp, xs, bp, bs, cache_k_a, cache_v_a, w_in_a, w_out_a,
                                                ln_g[i], ln_b[i], alpha, (j, n_a, sb_stack))
            new_p, names = (), ("ka_s", "va_s")
        elif kind == 1:
            xp, xs, new_p, new_s = _swa_layer(xp, xs, bp, bs, cache_k_b[j], cache_v_b[j], w_in_b[j], sinks_b[j],
                                              w_out_b[j], ln_g[i], ln_b[i], alpha)
            names = ("kb_p", "vb_p", "kb_s", "vb_s")
        else:
            xp, xs, new_p, new_s = _s5_layer(xp, xs, bp, bs, state_re_c[j], state_im_c[j], w_in_c[j], a_re_c[j],
                                             a_im_c[j], log_dt_c[j], b_re_c[j], b_im_c[j], c_re_c[j], c_im_c[j],
                                             d_c[j], w_glu_c[j], w_out_c[j], ln_g[i], ln_b[i], alpha)
            names = ("hr_p", "hi_p", "hr_s", "hi_s")
        for nm, val in zip(names, tuple(new_p) + tuple(new_s)):
            outs[nm].append(val)
    ka_p = sb_stack[0].reshape(n_a, bp, sp, heads_a, hd_a)
    va_p = sb_stack[1].reshape(n_a, bp, sp, heads_a, hd_a)
    return (xp.reshape(bp, sp, d), xs.reshape(bs, ss, d),
            ka_p, va_p, jnp.stack(outs["ka_s"]), jnp.stack(outs["va_s"]),
            jnp.stack(outs["kb_p"]), jnp.stack(outs["vb_p"]), jnp.stack(outs["kb_s"]), jnp.stack(outs["vb_s"]),
            jnp.stack(outs["hr_p"]), jnp.stack(outs["hi_p"]), jnp.stack(outs["hr_s"]), jnp.stack(outs["hi_s"]))
```
